```python
import jax, jax.numpy as jnp
from jax import lax
import numpy as np

D_MODEL = 1024
BATCH = 16
SEQ = 2048
DEPTH = 1

HEAD_DIM = 64
ROPE_DIM = HEAD_DIM // 4
ROPE_THETA = 500000.0
Q_BLOCK = 128
EPS = 1e-6
NEG_INF = -1e30
FORCED_SCORE = 1e6
NSA_HEADS = 8
NSA_KV_HEADS = 2
NSA_GROUP = NSA_HEADS // NSA_KV_HEADS
CMP_LEN = 32
CMP_STRIDE = 16
CMP_HIDDEN = 128
SLC_BLOCK = 64
SLC_TOPN = 8
WINDOW = 512
DSA_HEADS = 8
KV_LORA = 128
NOPE_DIM = HEAD_DIM - ROPE_DIM
IDX_HEADS = 8
IDX_DIM = 64
IDX_TOPK_MAX = 256
D_FF = ((8 * D_MODEL + 3 * 256 - 1) // (3 * 256)) * 256
IN_SPLITS = (NSA_HEADS * HEAD_DIM,) + (NSA_KV_HEADS * HEAD_DIM,) * 6 + (
    3 * NSA_HEADS, DSA_HEADS * HEAD_DIM, KV_LORA, ROPE_DIM, IDX_HEADS * IDX_DIM, IDX_DIM, IDX_HEADS)
IN_WIDTH = sum(IN_SPLITS)

kernel_name = 'hybrid_nsa_dsa_parallel_heads'


def _rms_norm(x, g):
    xf = x.astype(jnp.float32)
    y = xf * lax.rsqrt(jnp.mean(xf * xf, axis=-1, keepdims=True) + EPS)
    return (y * g.astype(jnp.float32)).astype(x.dtype)


def _rope_tables(seq):
    inv_freq = 1.0 / (ROPE_THETA ** (jnp.arange(0, ROPE_DIM, 2, dtype=jnp.float32) / ROPE_DIM))
    ang = jnp.arange(seq, dtype=jnp.float32)[:, None] * inv_freq[None, :]
    return jnp.cos(ang), jnp.sin(ang)


def _rope(x, cos, sin):
    half = ROPE_DIM // 2
    xr = x[..., :ROPE_DIM].astype(jnp.float32)
    x1, x2 = xr[..., :half], xr[..., half:]
    c, s = cos[:, None, :], sin[:, None, :]
    rot = jnp.concatenate([x1 * c - x2 * s, x2 * c + x1 * s], axis=-1)
    return jnp.concatenate([rot.astype(x.dtype), x[..., ROPE_DIM:]], axis=-1)


def _masked_softmax(s, mask):
    s = jnp.where(mask, s.astype(jnp.float32), NEG_INF)
    m = jnp.max(s, axis=-1, keepdims=True)
    e = jnp.where(mask, jnp.exp(s - m), 0.0)
    return e / jnp.maximum(jnp.sum(e, axis=-1, keepdims=True), 1e-30)


def _q_blocks(a):
    b, s = a.shape[:2]
    return jnp.moveaxis(a.reshape((b, s // Q_BLOCK, Q_BLOCK) + a.shape[2:]), 1, 0)


def _from_blocks(o):
    nb, b, tq = o.shape[:3]
    return jnp.moveaxis(o, 0, 1).reshape((b, nb * tq) + o.shape[3:])


def _compress(k, pe, w1, b1, w2, b2):
    b, s, g, dh = k.shape
    n_cmp = (s - CMP_LEN) // CMP_STRIDE + 1
    idx = np.arange(n_cmp)[:, None] * CMP_STRIDE + np.arange(CMP_LEN)[None, :]
    blocks = k[:, idx] + pe[:, None, :]
    flat = jnp.transpose(blocks, (0, 3, 1, 2, 4)).reshape(b, g, n_cmp, CMP_LEN * dh)
    hid = jax.nn.gelu(flat @ w1 + b1)
    return hid @ w2 + b2


def _nsa_group(q, k_cmp, v_cmp, k_slc, v_slc, k_win, v_win, gates, cos, sin,
               cmp_pe, cmp_w1, cmp_b1, cmp_w2, cmp_b2):
    b, s = q.shape[:2]
    g_, r_, dh = NSA_KV_HEADS, NSA_GROUP, HEAD_DIM
    scale = HEAD_DIM ** -0.5
    tpos = jnp.arange(s)
    qg = _rope(q.reshape(b, s, NSA_HEADS, dh), cos, sin).reshape(b, s, g_, r_, dh)
    kc = _compress(_rope(k_cmp.reshape(b, s, g_, dh), cos, sin), cmp_pe[0], cmp_w1[0], cmp_b1[0], cmp_w2[0], cmp_b2[0])
    vc = _compress(v_cmp.reshape(b, s, g_, dh), cmp_pe[1], cmp_w1[1], cmp_b1[1], cmp_w2[1], cmp_b2[1])
    n_cmp = kc.shape[2]
    cmp_end = jnp.arange(n_cmp) * CMP_STRIDE + CMP_LEN - 1
    mask_c = cmp_end[None, :] <= tpos[:, None]
    p_c = _masked_softmax(jnp.einsum('bsgrd,bgnd->bgrsn', qg, kc) * scale, mask_c)
    o_cmp = jnp.einsum('bgrsn,bgnd->bsgrd', p_c.astype(vc.dtype), vc).reshape(b, s, NSA_HEADS, dh)
    n_slc = s // SLC_BLOCK
    ci = np.arange(n_cmp)[:, None] * CMP_STRIDE
    sj = np.arange(n_slc)[None, :] * SLC_BLOCK
    overlap = ((ci < sj + SLC_BLOCK) & (ci + CMP_LEN > sj)).astype(np.float32)
    imp = jnp.einsum('bgrsn,nj->bgsj', p_c, jnp.asarray(overlap))
    blk_t = (tpos // SLC_BLOCK)[:, None]
    j = jnp.arange(n_slc)[None, :]
    visible = j <= blk_t
    forced = (j == 0) | (j == blk_t) | (j == blk_t - 1)
    imp = jnp.where(visible, jnp.where(forced, FORCED_SCORE, imp), -jnp.inf)
    n_sel = min(SLC_TOPN, n_slc)
    _, sel = lax.top_k(imp, n_sel)
    ks_blocks = jnp.transpose(_rope(k_slc.reshape(b, s, g_, dh), cos, sin).reshape(b, n_slc, SLC_BLOCK, g_, dh), (0, 3, 1, 2, 4))
    vs_blocks = jnp.transpose(v_slc.reshape(b, n_slc, SLC_BLOCK, g_, dh), (0, 3, 1, 2, 4))
    pad = ((0, 0), (WINDOW, 0), (0, 0), (0, 0))
    kw = jnp.pad(_rope(k_win.reshape(b, s, g_, dh), cos, sin), pad)
    vw = jnp.pad(v_win.reshape(b, s, g_, dh), pad)
    gather = jax.vmap(jax.vmap(lambda tab, ix: tab[ix]))

    def block_fn(args):
        q_b, sel_b, t0 = args
        sel_b = jnp.transpose(sel_b, (0, 2, 1, 3))
        tq = t0 + jnp.arange(Q_BLOCK)
        kg = gather(ks_blocks, sel_b).reshape(b, g_, Q_BLOCK, n_sel * SLC_BLOCK, dh)
        vg = gather(vs_blocks, sel_b).reshape(b, g_, Q_BLOCK, n_sel * SLC_BLOCK, dh)
        kpos = (sel_b[..., None] * SLC_BLOCK + jnp.arange(SLC_BLOCK)).reshape(b, g_, Q_BLOCK, n_sel * SLC_BLOCK)
        mask_s = (kpos <= tq[None, None, :, None])[:, :, None]
        p_s = _masked_softmax(jnp.einsum('btgrd,bgtkd->bgrtk', q_b, kg) * scale, mask_s)
        o_s = jnp.einsum('bgrtk,bgtkd->btgrd', p_s.astype(vg.dtype), vg)
        kwb = lax.dynamic_slice_in_dim(kw, t0, Q_BLOCK + WINDOW, axis=1)
        vwb = lax.dynamic_slice_in_dim(vw, t0, Q_BLOCK + WINDOW, axis=1)
        wpos = t0 - WINDOW + jnp.arange(Q_BLOCK + WINDOW)
        mask_w = (wpos[None, :] <= tq[:, None]) & (wpos[None, :] > tq[:, None] - WINDOW) & (wpos[None, :] >= 0)
        p_w = _masked_softmax(jnp.einsum('btgrd,bsgd->bgrts', q_b, kwb) * scale, mask_w)
        o_w = jnp.einsum('bgrts,bsgd->btgrd', p_w.astype(vwb.dtype), vwb)
        return o_s, o_w

    starts = jnp.arange(s // Q_BLOCK, dtype=jnp.int32) * Q_BLOCK
    o_s, o_w = lax.map(block_fn, (_q_blocks(qg), _q_blocks(jnp.transpose(sel, (0, 2, 1, 3))), starts))
    o_s = _from_blocks(o_s).reshape(b, s, NSA_HEADS, dh)
    o_w = _from_blocks(o_w).reshape(b, s, NSA_HEADS, dh)
    gt = jax.nn.sigmoid(gates.astype(jnp.float32)).astype(q.dtype).reshape(b, s, 3, NSA_HEADS, 1)
    o = gt[:, :, 0] * o_cmp + gt[:, :, 1] * o_s + gt[:, :, 2] * o_w
    return o.reshape(b, s, NSA_HEADS * dh)


def _dsa_group(q, c_kv, k_rope, q_idx, k_idx, w_idx, cos, sin, g_kv, w_uk, w_uv):
    b, s = q.shape[:2]
    scale = HEAD_DIM ** -0.5
    q = _rope(q.reshape(b, s, DSA_HEADS, HEAD_DIM), cos, sin)
    q_rope, q_nope = q[..., :ROPE_DIM], q[..., ROPE_DIM:]
    c_kv = _rms_norm(c_kv, g_kv)
    k_rope = _rope(k_rope[:, :, None, :], cos, sin)[:, :, 0]
    q_lat = jnp.einsum('bshe,hre->bshr', q_nope, w_uk)
    q_idx = _rope(q_idx.reshape(b, s, IDX_HEADS, IDX_DIM), cos, sin)
    k_idx = _rope(k_idx[:, :, None, :], cos, sin)[:, :, 0]
    w_idx = w_idx.astype(jnp.float32) * (IDX_HEADS ** -0.5 * IDX_DIM ** -0.5)
    k_sel = min(IDX_TOPK_MAX, s // 4)
    spos = jnp.arange(s)
    gather = jax.vmap(lambda tab, ix: tab[ix])

    def block_fn(args):
        ql_b, qr_b, qi_b, wi_b, t0 = args
        tq = t0 + jnp.arange(Q_BLOCK)
        logits = jnp.einsum('bthd,bsd->bths', qi_b, k_idx).astype(jnp.float32)
        score = jnp.einsum('bths,bth->bts', jax.nn.relu(logits), wi_b)
        score = jnp.where(spos[None, None, :] <= tq[None, :, None], score, -jnp.inf)
        _, idx = lax.top_k(score, k_sel)
        cg = gather(c_kv, idx)
        krg = gather(k_rope, idx)
        sc = (jnp.einsum('bthr,btkr->bhtk', ql_b, cg) + jnp.einsum('bthe,btke->bhtk', qr_b, krg)) * scale
        mask = (idx <= tq[None, :, None])[:, None]
        p = _masked_softmax(sc, mask)
        return jnp.einsum('bhtk,btkr->bthr', p.astype(cg.dtype), cg)

    starts = jnp.arange(s // Q_BLOCK, dtype=jnp.int32) * Q_BLOCK
    o_lat = _from_blocks(lax.map(block_fn, (_q_blocks(q_lat), _q_blocks(q_rope), _q_blocks(q_idx), _q_blocks(w_idx), starts)))
    o = jnp.einsum('bshr,hrd->bshd', o_lat, w_uv)
    return o.reshape(b, s, DSA_HEADS * HEAD_DIM)


def _hybrid_mixer(h, cos, sin, w_in, cmp_pe, cmp_w1, cmp_b1, cmp_w2, cmp_b2, g_kv, w_uk, w_uv, w_out):
    proj = h @ w_in
    split_points = [int(v) for v in np.cumsum(IN_SPLITS)[:-1]]
    (nq, nkc, nvc, nks, nvs, nkw, nvw, ngate, dq, dckv, dkr, iq, ik, iw) = jnp.split(proj, split_points, axis=-1)
    o_a = _nsa_group(nq, nkc, nvc, nks, nvs, nkw, nvw, ngate, cos, sin, cmp_pe, cmp_w1, cmp_b1, cmp_w2, cmp_b2)
    o_b = _dsa_group(dq, dckv, dkr, iq, ik, iw, cos, sin, g_kv, w_uk, w_uv)
    return jnp.concatenate([o_a, o_b], axis=-1) @ w_out


def _swiglu(h, w_gate_up, w_down):
    gate, up = jnp.split(h @ w_gate_up, 2, axis=-1)
    return (jax.nn.silu(gate) * up) @ w_down


def setup_inputs(seed: int = 0) -> dict:
    key = jax.random.key(seed)
    ks = jax.random.split(key, 20)
    L = DEPTH

    def nrm(k, shape, scale):
        return jax.random.normal(k, shape, jnp.float32) * scale

    def gain(k, shape):
        return 1.0 + 0.05 * jax.random.normal(k, shape, jnp.float32)

    return {
        'x': nrm(ks[0], (BATCH, SEQ, D_MODEL), 1.0),
        'c': nrm(ks[1], (BATCH, D_MODEL), 1.0),
        'w_ada': nrm(ks[2], (L, D_MODEL, 6 * D_MODEL), 0.5 * D_MODEL ** -0.5),
        'b_ada': nrm(ks[3], (L, 6 * D_MODEL), 0.01),
        'g_pre_mix': gain(ks[4], (L, D_MODEL)),
        'g_post_mix': gain(ks[5], (L, D_MODEL)),
        'g_pre_ffn': gain(ks[6], (L, D_MODEL)),
        'g_post_ffn': gain(ks[7], (L, D_MODEL)),
        'w_in': nrm(ks[8], (L, D_MODEL, IN_WIDTH), D_MODEL ** -0.5),
        'cmp_pe': nrm(ks[9], (L, 2, CMP_LEN, HEAD_DIM), 0.1),
        'cmp_w1': nrm(ks[10], (L, 2, CMP_LEN * HEAD_DIM, CMP_HIDDEN), (CMP_LEN * HEAD_DIM) ** -0.5),
        'cmp_b1': nrm(ks[11], (L, 2, CMP_HIDDEN), 0.01),
        'cmp_w2': nrm(ks[12], (L, 2, CMP_HIDDEN, HEAD_DIM), CMP_HIDDEN ** -0.5),
        'cmp_b2': nrm(ks[13], (L, 2, HEAD_DIM), 0.01),
        'g_kv_norm': gain(ks[14], (L, KV_LORA)),
        'w_uk': nrm(ks[15], (L, DSA_HEADS, KV_LORA, NOPE_DIM), KV_LORA ** -0.5),
        'w_uv': nrm(ks[16], (L, DSA_HEADS, KV_LORA, HEAD_DIM), KV_LORA ** -0.5),
        'w_out': nrm(ks[17], (L, D_MODEL, D_MODEL), D_MODEL ** -0.5),
        'w_gate_up': nrm(ks[18], (L, D_MODEL, 2 * D_FF), D_MODEL ** -0.5),
        'w_down': nrm(ks[19], (L, D_FF, D_MODEL), D_FF ** -0.5),
    }


def reference(x, c, w_ada, b_ada, g_pre_mix, g_post_mix, g_pre_ffn, g_post_ffn, w_in,
              cmp_pe, cmp_w1, cmp_b1, cmp_w2, cmp_b2, g_kv_norm, w_uk, w_uv, w_out,
              w_gate_up, w_down):
    cos, sin = _rope_tables(x.shape[1])
    c_act = jax.nn.silu(c)
    for l in range(DEPTH):
        mod = (c_act @ w_ada[l] + b_ada[l])[:, None, :]
        sh_m, sc_m, ga_m, sh_f, sc_f, ga_f = jnp.split(mod, 6, axis=-1)
        h = _rms_norm(x, g_pre_mix[l]) * (1.0 + sc_m) + sh_m
        y = _hybrid_mixer(h, cos, sin, w_in[l], cmp_pe[l], cmp_w1[l], cmp_b1[l], cmp_w2[l], cmp_b2[l],
                          g_kv_norm[l], w_uk[l], w_uv[l], w_out[l])
        x = x + ga_m * _rms_norm(y, g_post_mix[l])
        h = _rms_norm(x, g_pre_ffn[l]) * (1.0 + sc_f) + sh_f
        x = x + ga_f * _rms_norm(_swiglu(h, w_gate_up[l], w_down[l]), g_post_ffn[l])
    return x
```

```python
import functools

import numpy as np
import jax
import jax.numpy as jnp
from jax import lax
from jax.experimental import pallas as pl
from jax.experimental.pallas import tpu as pltpu

F32 = jnp.float32
I32 = jnp.int32
MXU_DTYPE = jnp.bfloat16

HEAD_DIM = 64
ROPE_DIM = HEAD_DIM // 4
ROPE_HALF = ROPE_DIM // 2
ROPE_THETA = 500000.0
EPS = 1e-6
NEG_INF = -1e30
FORCED_SCORE = 1e6
NSA_HEADS = 8
NSA_KV_HEADS = 2
NSA_GROUP = NSA_HEADS // NSA_KV_HEADS
CMP_LEN = 32
CMP_STRIDE = 16
CMP_HIDDEN = 128
SLC_BLOCK = 64
SLC_TOPN = 8
WINDOW = 512
DSA_HEADS = 8
KV_LORA = 128
NOPE_DIM = HEAD_DIM - ROPE_DIM
IDX_HEADS = 8
IDX_DIM = 64
IDX_TOPK_MAX = 256

LANES = 128
TQ = 128
TK = 128
TM_PROJ = 512
TM_FFN = 512
FF_CHUNK = 256
INT_MIN = -(2 ** 31)
VMEM_LIMIT = 56 * 1024 * 1024

MISC_IK = 0
MISC_KR = 64
MISC_GATE = 80
MISC_IW = 104


def _dot(a, b):
    return jnp.dot(a, b, preferred_element_type=F32)


def _dot_nt(a, b):
    return lax.dot_general(a, b, (((1,), (1,)), ((), ())), preferred_element_type=F32)


def _sigmoid(v):
    return 1.0 / (1.0 + jnp.exp(-v))


def _roll_lanes(v, shift):
    return pltpu.roll(v, shift, 1)


def _swap_halves(v):
    return _roll_lanes(v.astype(F32), LANES // 2).astype(v.dtype)


def _ada_kernel(c_ref, w_ref, b_ref, o_ref):
    c = c_ref[...]
    act = (c * _sigmoid(c)).astype(MXU_DTYPE)
    o_ref[...] = _dot(act, w_ref[...].astype(MXU_DTYPE)) + b_ref[...]


def _ada_call(c, w_ada, b_ada):
    bsz, d = c.shape
    n = w_ada.shape[1]
    tn = d
    return pl.pallas_call(
        _ada_kernel,
        grid=(n // tn,),
        in_specs=[pl.BlockSpec((bsz, d), lambda j: (0, 0)),
                  pl.BlockSpec((d, tn), lambda j: (0, j)),
                  pl.BlockSpec((1, tn), lambda j: (0, j))],
        out_specs=pl.BlockSpec((bsz, tn), lambda j: (0, j)),
        out_shape=jax.ShapeDtypeStruct((bsz, n), F32),
        name="adaln_mod",
    )(c, w_ada, b_ada.reshape(1, n))


G_NQ = (0, 512)
G_NCV = (512, 768)
G_NKV = (768, 1280)
G_DQ = (1280, 1792)
G_IQ = (1792, 2304)
G_CKV = (2304, 2432)
G_MISC = (2432, 2560)
IN_PERM_WIDTH = 2560


def _rope_tile(v, cos, sin_a, sin_b):
    return v * cos + _roll_lanes(v, LANES - ROPE_HALF) * sin_a + _roll_lanes(v, ROPE_HALF) * sin_b


def _inproj_kernel(x_ref, mod_ref, g_ref, gkv_ref, w_ref, cos_ref, sa_ref, sb_ref,
                   nq_ref, ncv_ref, nkv_ref, dq_ref, iq_ref, kcat_ref, kidx_ref, misc_ref):
    x = x_ref[0]
    y = x * lax.rsqrt(jnp.mean(x * x, axis=-1, keepdims=True) + EPS) * g_ref[...]
    shift = mod_ref[0, 0:1, :]
    scale = mod_ref[0, 1:2, :]
    h = (y * (1.0 + scale) + shift).astype(MXU_DTYPE)
    cos, sin_a, sin_b = cos_ref[...], sa_ref[...], sb_ref[...]
    qscale = HEAD_DIM ** -0.5

    def proj(lo, hi):
        return _dot(h, w_ref[:, lo:hi])

    def rope_cols(v, roped):
        tiles = []
        for j in range(v.shape[1] // LANES):
            t = v[:, j * LANES:(j + 1) * LANES]
            tiles.append(_rope_tile(t, cos, sin_a, sin_b) if roped[j] else t)
        return tiles

    for j, t in enumerate(rope_cols(proj(*G_NQ), [True] * 4)):
        nq_ref[0, :, j * LANES:(j + 1) * LANES] = (t * qscale).astype(nq_ref.dtype)
    for j, t in enumerate(rope_cols(proj(*G_NCV), [True, False])):
        ncv_ref[0, :, j * LANES:(j + 1) * LANES] = t
    for j, t in enumerate(rope_cols(proj(*G_NKV), [True, False, True, False])):
        nkv_ref[0, :, j * LANES:(j + 1) * LANES] = t.astype(nkv_ref.dtype)
    for j, t in enumerate(rope_cols(proj(*G_DQ), [True] * 4)):
        dq_ref[0, :, j * LANES:(j + 1) * LANES] = (t * qscale).astype(dq_ref.dtype)
    for j, t in enumerate(rope_cols(proj(*G_IQ), [True] * 4)):
        iq_ref[0, :, j * LANES:(j + 1) * LANES] = t.astype(iq_ref.dtype)

    ckv = proj(*G_CKV)
    ckv = ckv * lax.rsqrt(jnp.mean(ckv * ckv, axis=-1, keepdims=True) + EPS) * gkv_ref[...]
    misc = _rope_tile(proj(*G_MISC), cos, sin_a, sin_b)
    misc_ref[0] = misc
    lane = lax.broadcasted_iota(I32, misc.shape, 1)
    kcat_ref[0, :, 0:LANES] = ckv.astype(kcat_ref.dtype)
    kr = jnp.where(lane < ROPE_DIM, _roll_lanes(misc, LANES - MISC_KR), 0.0)
    kcat_ref[0, :, LANES:2 * LANES] = kr.astype(kcat_ref.dtype)
    ik = jnp.where(lane < IDX_DIM, misc, 0.0)
    kidx_ref[0, :, 0:LANES] = ik.astype(kidx_ref.dtype)
    kidx_ref[0, :, LANES:2 * LANES] = _roll_lanes(ik, LANES // 2).astype(kidx_ref.dtype)


def _inproj_call(x, mod3, g_pre, g_kv, w_perm, cos_t, sa_t, sb_t):
    bsz, s, d = x.shape
    tm = min(TM_PROJ, s)
    row = lambda width: pl.BlockSpec((1, tm, width), lambda b, i: (b, i, 0))
    const2 = lambda shape: pl.BlockSpec(shape, lambda b, i: (0, 0))
    tab = pl.BlockSpec((tm, LANES), lambda b, i: (i, 0))
    widths = [(512, MXU_DTYPE), (256, F32), (512, MXU_DTYPE), (512, MXU_DTYPE), (512, MXU_DTYPE),
              (256, MXU_DTYPE), (256, MXU_DTYPE), (LANES, F32)]
    return pl.pallas_call(
        _inproj_kernel,
        grid=(bsz, s // tm),
        in_specs=[row(d),
                  pl.BlockSpec((1, 6, d), lambda b, i: (b, 0, 0)),
                  const2((1, d)), const2((1, KV_LORA)), const2((d, IN_PERM_WIDTH)),
                  tab, tab, tab],
        out_specs=[row(w) for w, _ in widths],
        out_shape=[jax.ShapeDtypeStruct((bsz, s, w), dt) for w, dt in widths],
        compiler_params=pltpu.CompilerParams(vmem_limit_bytes=VMEM_LIMIT),
        name="inproj_rope",
    )(x, mod3, g_pre, g_kv, w_perm, cos_t, sa_t, sb_t)


def _gelu_tanh(v):
    return 0.5 * v * (1.0 + jnp.tanh(np.sqrt(2.0 / np.pi) * (v + 0.044715 * (v * v * v))))


def _compress_kernel(x_ref, w1a_ref, w1b_ref, pea_ref, peb_ref, b1_ref, w2_ref, b2_ref, o_ref):
    n_chunks = x_ref.shape[1]
    for kv in range(2):
        first = jnp.zeros((n_chunks, 2 * CMP_HIDDEN), F32)
        second = jnp.zeros((n_chunks, 2 * CMP_HIDDEN), F32)
        for l in range(CMP_STRIDE):
            lo = l * 2 * LANES + kv * LANES
            slab = x_ref[0, :, lo:lo + LANES]
            first = first + _dot((slab + pea_ref[kv, l]).astype(MXU_DTYPE), w1a_ref[kv, l])
            second = second + _dot((slab + peb_ref[kv, l]).astype(MXU_DTYPE), w1b_ref[kv, l])
        hid = first + pltpu.roll(second, n_chunks - 1, 0) + b1_ref[kv]
        out = _dot(_gelu_tanh(hid).astype(MXU_DTYPE), w2_ref[kv]) + b2_ref[kv]
        o_ref[0, :, kv * LANES:(kv + 1) * LANES] = out.astype(o_ref.dtype)


def _compress_call(ncv, w1a, w1b, pea, peb, b1, w2, b2):
    bsz, s, _ = ncv.shape
    n_chunks = s // CMP_STRIDE
    xr = ncv.reshape(bsz, n_chunks, CMP_STRIDE * 2 * LANES)
    full = lambda a: pl.BlockSpec(a.shape, lambda b: (0,) * a.ndim)
    return pl.pallas_call(
        _compress_kernel,
        grid=(bsz,),
        in_specs=[pl.BlockSpec((1, n_chunks, CMP_STRIDE * 2 * LANES), lambda b: (b, 0, 0)),
                  full(w1a), full(w1b), full(pea), full(peb), full(b1), full(w2), full(b2)],
        out_specs=pl.BlockSpec((1, n_chunks, 2 * LANES), lambda b: (b, 0, 0)),
        out_shape=jax.ShapeDtypeStruct((bsz, n_chunks, 2 * LANES), MXU_DTYPE),
        compiler_params=pltpu.CompilerParams(vmem_limit_bytes=VMEM_LIMIT),
        name="nsa_compress",
    )(xr, w1a, w1b, pea, peb, b1, w2, b2)


def _split3(v):
    hi = v.astype(MXU_DTYPE)
    r1 = v - hi.astype(F32)
    mid = r1.astype(MXU_DTYPE)
    lo = (r1 - mid.astype(F32)).astype(MXU_DTYPE)
    return hi, mid, lo


def _online_init(m_ref, l_ref, acc_ref):
    m_ref[...] = jnp.full(m_ref.shape, NEG_INF, F32)
    l_ref[...] = jnp.zeros(l_ref.shape, F32)
    acc_ref[...] = jnp.zeros(acc_ref.shape, F32)


def _online_step(s, mask, v, m_ref, l_ref, acc_ref):
    nh = m_ref.shape[0]
    s3 = jnp.where(mask[None], s.reshape(nh, TQ, TK), NEG_INF)
    m_old = m_ref[...]
    m_new = jnp.maximum(m_old, jnp.max(s3, axis=-1, keepdims=True))
    alpha = jnp.exp(m_old - m_new)
    e = jnp.where(mask[None], jnp.exp(s3 - m_new), 0.0)
    l_ref[...] = alpha * l_ref[...] + jnp.sum(e, axis=-1, keepdims=True)
    pv = _dot(e.reshape(nh * TQ, TK).astype(MXU_DTYPE), v)
    acc_ref[...] = alpha * acc_ref[...] + pv.reshape(nh, TQ, LANES)
    m_ref[...] = m_new


def _online_result(l_ref, acc_ref):
    return acc_ref[...] / jnp.maximum(l_ref[...], 1e-30)


def _nsa_kernel(q_ref, kv_ref, kcv_ref, misc_ref, ov_ref, eful_ref, o_ref,
                selexp_ref, m_ref, l_ref, acc_ref):
    qb = pl.program_id(1)
    n_kt = selexp_ref.shape[0]
    n_slc = n_kt * TK // SLC_BLOCK
    row = lax.broadcasted_iota(I32, (TQ, LANES), 0)
    lane = lax.broadcasted_iota(I32, (TQ, LANES), 1)
    t_abs = qb * TQ + row
    gates = _sigmoid(misc_ref[0])
    half = LANES // 2

    for g in range(NSA_KV_HEADS):
        in_g = (lane >= half * g) & (lane < half * (g + 1))
        tiles = []
        for hh in range(NSA_GROUP):
            col = (g * NSA_GROUP + hh) // 2 * LANES
            t = q_ref[0, :, col:col + LANES]
            if hh % 2 != g:
                t = _swap_halves(t)
            tiles.append(jnp.where(in_g, t, jnp.zeros_like(t)))
        qs = jnp.concatenate(tiles, axis=0)

        kc = kcv_ref[0, :, 0:LANES]
        vc = kcv_ref[0, :, LANES:2 * LANES]
        mask_c = (lane * CMP_STRIDE + (CMP_LEN - 1)) <= t_abs
        s3 = jnp.where(mask_c[None], _dot_nt(qs, kc).reshape(NSA_GROUP, TQ, LANES), NEG_INF)
        mx = jnp.max(s3, axis=-1, keepdims=True)
        e = jnp.where(mask_c[None], jnp.exp(s3 - mx), 0.0)
        p_c = e / jnp.maximum(jnp.sum(e, axis=-1, keepdims=True), 1e-30)
        o_c = _dot(p_c.reshape(NSA_GROUP * TQ, LANES).astype(MXU_DTYPE), vc).reshape(NSA_GROUP, TQ, LANES)

        psum = jnp.sum(p_c, axis=0)
        ov = ov_ref[...]
        imp = sum(_dot(piece, ov) for piece in _split3(psum))
        blk_t = t_abs // SLC_BLOCK
        visible = lane <= blk_t
        forced = (lane == 0) | (lane == blk_t) | (lane == blk_t - 1)
        imp = jnp.where(visible, jnp.where(forced, FORCED_SCORE, imp), -jnp.inf)
        rank = jnp.zeros((TQ, LANES), I32)
        for jp in range(n_slc):
            colv = imp[:, jp:jp + 1]
            beats = (colv > imp) | ((colv == imp) & (lane > jp))
            rank = rank + jnp.where(beats, 1, 0)
        sel = jnp.where((rank < min(SLC_TOPN, n_slc)) & (lane < n_slc), 1.0, 0.0).astype(MXU_DTYPE)
        selexp = _dot(sel, eful_ref[...])
        for kt in range(n_kt):
            selexp_ref[kt] = selexp[:, kt * TK:(kt + 1) * TK]

        def kv_tile(kt, col):
            start = pl.multiple_of(kt * TK, TK)
            return kv_ref[0, pl.ds(start, TK), col:col + LANES]

        def sel_body(kt, carry):
            kpos = kt * TK + lane
            mask = (selexp_ref[kt] > 0.5) & (kpos <= t_abs)
            _online_step(_dot_nt(qs, kv_tile(kt, 0)), mask, kv_tile(kt, LANES), m_ref, l_ref, acc_ref)
            return carry

        _online_init(m_ref, l_ref, acc_ref)
        lax.fori_loop(0, qb + 1, sel_body, 0)
        o_s = _online_result(l_ref, acc_ref)

        def win_body(kt, carry):
            kpos = kt * TK + lane
            mask = (kpos <= t_abs) & (kpos > t_abs - WINDOW)
            _online_step(_dot_nt(qs, kv_tile(kt, 2 * LANES)), mask, kv_tile(kt, 3 * LANES),
                         m_ref, l_ref, acc_ref)
            return carry

        _online_init(m_ref, l_ref, acc_ref)
        lax.fori_loop(jnp.maximum(qb - WINDOW // TK, 0), qb + 1, win_body, 0)
        o_w = _online_result(l_ref, acc_ref)

        heads = []
        for hh in range(NSA_GROUP):
            hd = g * NSA_GROUP + hh
            gc = gates[:, MISC_GATE + hd:MISC_GATE + hd + 1]
            gs = gates[:, MISC_GATE + NSA_HEADS + hd:MISC_GATE + NSA_HEADS + hd + 1]
            gw = gates[:, MISC_GATE + 2 * NSA_HEADS + hd:MISC_GATE + 2 * NSA_HEADS + hd + 1]
            heads.append(gc * o_c[hh] + gs * o_s[hh] + gw * o_w[hh])
        for pair in range(NSA_GROUP // 2):
            even, odd = heads[2 * pair], heads[2 * pair + 1]
            if g == 0:
                odd = _roll_lanes(odd, half)
            else:
                even = _roll_lanes(even, half)
            col = (g * NSA_GROUP // 2 + pair) * LANES
            o_ref[0, :, col:col + LANES] = jnp.where(lane < half, even, odd).astype(o_ref.dtype)


def _nsa_call(nq, nkv, kcv, misc, ov, eful):
    bsz, s, _ = nq.shape
    n_kt = s // TK
    rowspec = lambda width: pl.BlockSpec((1, TQ, width), lambda b, i: (b, i, 0))
    batch = lambda a: pl.BlockSpec((1,) + a.shape[1:], lambda b, i: (b, 0, 0))
    const = lambda a: pl.BlockSpec(a.shape, lambda b, i: (0, 0))
    return pl.pallas_call(
        _nsa_kernel,
        grid=(bsz, s // TQ),
        in_specs=[rowspec(512), batch(nkv), batch(kcv), rowspec(LANES), const(ov), const(eful)],
        out_specs=rowspec(512),
        out_shape=jax.ShapeDtypeStruct((bsz, s, 512), MXU_DTYPE),
        scratch_shapes=[pltpu.VMEM((n_kt, TQ, TK), F32),
                        pltpu.VMEM((NSA_GROUP, TQ, 1), F32),
                        pltpu.VMEM((NSA_GROUP, TQ, 1), F32),
                        pltpu.VMEM((NSA_GROUP, TQ, LANES), F32)],
        compiler_params=pltpu.CompilerParams(vmem_limit_bytes=VMEM_LIMIT),
        name="nsa_attention",
    )(nq, nkv, kcv, misc, ov, eful)


def _dsa_kernel(dq_ref, iq_ref, misc_ref, kidx_ref, kcat_ref, wq_ref, wuv_ref, tri_ref, o_ref,
                keys_ref, qcat_ref, m_ref, l_ref, acc_ref, *, k_sel):
    qb = pl.program_id(1)
    n_kt = qb + 1
    row = lax.broadcasted_iota(I32, (TQ, LANES), 0)
    lane = lax.broadcasted_iota(I32, (TQ, LANES), 1)
    t_abs = qb * TQ + row

    for pair in range(DSA_HEADS // 2):
        qc = _dot(dq_ref[0, :, pair * LANES:(pair + 1) * LANES], wq_ref[pair]).astype(MXU_DTYPE)
        qcat_ref[(2 * pair) * TQ:(2 * pair + 1) * TQ, :] = qc[:, 0:2 * LANES]
        qcat_ref[(2 * pair + 1) * TQ:(2 * pair + 2) * TQ, :] = qc[:, 2 * LANES:4 * LANES]

    w_idx = misc_ref[0] * (IDX_HEADS ** -0.5 * IDX_DIM ** -0.5)

    def idx_body(kt, carry):
        start = pl.multiple_of(kt * TK, TK)
        score = jnp.zeros((TQ, TK), F32)
        for hd in range(IDX_HEADS):
            qt = iq_ref[0, :, (hd // 2) * LANES:(hd // 2 + 1) * LANES]
            kk = kidx_ref[0, pl.ds(start, TK), (hd % 2) * LANES:(hd % 2 + 1) * LANES]
            logit = _dot_nt(qt, kk)
            score = score + w_idx[:, MISC_IW + hd:MISC_IW + hd + 1] * jnp.maximum(logit, 0.0)
        bits = lax.bitcast_convert_type(score, I32)
        key = bits ^ ((bits >> 31) & 0x7FFFFFFF)
        key = jnp.where(score == 0.0, 0, key)
        keys_ref[kt] = jnp.where(kt * TK + lane <= t_abs, key, INT_MIN)
        return carry

    lax.fori_loop(0, n_kt, idx_body, 0)

    def count(pred):
        def body(kt, a):
            return a + jnp.where(pred(keys_ref[kt]), 1.0, 0.0)
        part = lax.fori_loop(0, n_kt, body, jnp.zeros((TQ, TK), F32))
        return jnp.sum(part, axis=-1, keepdims=True)

    kf = float(k_sel)
    zero = jnp.zeros((TQ, 1), I32)
    thr = jnp.where(count(lambda k: k >= zero) >= kf, zero, jnp.full((TQ, 1), INT_MIN, I32))

    def bit_body(i, thr):
        cand = thr | (1 << (30 - i))
        return jnp.where(count(lambda k: k >= cand) >= kf, cand, thr)

    thr = lax.fori_loop(0, 31, bit_body, thr)
    thr = jnp.maximum(thr, INT_MIN + 1)
    need = kf - count(lambda k: k > thr)

    def att_body(kt, run):
        start = pl.multiple_of(kt * TK, TK)
        key = keys_ref[kt]
        eq = key == thr
        eqf = jnp.where(eq, 1.0, 0.0)
        prefix = _dot(eqf.astype(MXU_DTYPE), tri_ref[...])
        mask = (key > thr) | (eq & (run + prefix <= need))
        kc = kcat_ref[0, pl.ds(start, TK), :]
        _online_step(_dot_nt(qcat_ref[...], kc), mask, kc[:, 0:LANES], m_ref, l_ref, acc_ref)
        return run + jnp.sum(eqf, axis=-1, keepdims=True)

    _online_init(m_ref, l_ref, acc_ref)
    lax.fori_loop(0, n_kt, att_body, jnp.zeros((TQ, 1), F32))
    o_lat = _online_result(l_ref, acc_ref)

    for pair in range(DSA_HEADS // 2):
        both = jnp.concatenate([o_lat[2 * pair], o_lat[2 * pair + 1]], axis=1).astype(MXU_DTYPE)
        o_ref[0, :, pair * LANES:(pair + 1) * LANES] = _dot(both, wuv_ref[pair]).astype(o_ref.dtype)


def _dsa_call(dq, iq, misc, kidx, kcat, wq, wuv, tri):
    bsz, s, _ = dq.shape
    n_kt = s // TK
    k_sel = min(IDX_TOPK_MAX, s // 4)
    rowspec = lambda width: pl.BlockSpec((1, TQ, width), lambda b, i: (b, i, 0))
    batch = lambda a: pl.BlockSpec((1,) + a.shape[1:], lambda b, i: (b, 0, 0))
    const = lambda a: pl.BlockSpec(a.shape, lambda b, i: (0,) * a.ndim)
    return pl.pallas_call(
        functools.partial(_dsa_kernel, k_sel=k_sel),
        grid=(bsz, s // TQ),
        in_specs=[rowspec(512), rowspec(512), rowspec(LANES), batch(kidx), batch(kcat),
                  const(wq), const(wuv), const(tri)],
        out_specs=rowspec(512),
        out_shape=jax.ShapeDtypeStruct((bsz, s, 512), MXU_DTYPE),
        scratch_shapes=[pltpu.VMEM((n_kt, TQ, TK), I32),
                        pltpu.VMEM((DSA_HEADS * TQ, 2 * LANES), MXU_DTYPE),
                        pltpu.VMEM((DSA_HEADS, TQ, 1), F32),
                        pltpu.VMEM((DSA_HEADS, TQ, 1), F32),
                        pltpu.VMEM((DSA_HEADS, TQ, LANES), F32)],
        compiler_params=pltpu.CompilerParams(vmem_limit_bytes=VMEM_LIMIT),
        name="dsa_attention",
    )(dq, iq, misc, kidx, kcat, wq, wuv, tri)


def _rms(v, g):
    return v * lax.rsqrt(jnp.mean(v * v, axis=-1, keepdims=True) + EPS) * g


def _ffn_kernel(x_ref, oa_ref, ob_ref, mod_ref, gpm_ref, gpf_ref, gqf_ref, wout_ref, wgu_ref, wd_ref,
                o_ref, *, d_ff):
    x = x_ref[0]
    half = oa_ref.shape[2]
    y = _dot(oa_ref[0], wout_ref[0:half, :]) + _dot(ob_ref[0], wout_ref[half:2 * half, :])
    x1 = x + mod_ref[0, 2:3, :] * _rms(y, gpm_ref[...])
    h = (_rms(x1, gpf_ref[...]) * (1.0 + mod_ref[0, 4:5, :]) + mod_ref[0, 3:4, :]).astype(MXU_DTYPE)
    acc = jnp.zeros(x.shape, F32)
    for c in range(d_ff // FF_CHUNK):
        lo = c * FF_CHUNK
        gate = _dot(h, wgu_ref[:, lo:lo + FF_CHUNK])
        up = _dot(h, wgu_ref[:, d_ff + lo:d_ff + lo + FF_CHUNK])
        act = (gate * _sigmoid(gate) * up).astype(MXU_DTYPE)
        acc = acc + _dot(act, wd_ref[lo:lo + FF_CHUNK, :])
    o_ref[0] = x1 + mod_ref[0, 5:6, :] * _rms(acc, gqf_ref[...])


def _ffn_call(x, o_a, o_b, mod3, g_post_mix, g_pre_ffn, g_post_ffn, w_out, w_gu, w_down):
    bsz, s, d = x.shape
    d_ff = w_down.shape[0]
    tm = min(TM_FFN, s)
    row = lambda width: pl.BlockSpec((1, tm, width), lambda b, i: (b, i, 0))
    const = lambda a: pl.BlockSpec(a.shape, lambda b, i: (0, 0), pipeline_mode=pl.Buffered(1))
    return pl.pallas_call(
        functools.partial(_ffn_kernel, d_ff=d_ff),
        grid=(bsz, s // tm),
        in_specs=[row(d), row(o_a.shape[2]), row(o_b.shape[2]),
                  pl.BlockSpec((1, 6, d), lambda b, i: (b, 0, 0)),
                  const(g_post_mix), const(g_pre_ffn), const(g_post_ffn),
                  const(w_out), const(w_gu), const(w_down)],
        out_specs=row(d),
        out_shape=jax.ShapeDtypeStruct((bsz, s, d), x.dtype),
        compiler_params=pltpu.CompilerParams(vmem_limit_bytes=VMEM_LIMIT),
        name="outproj_ffn",
    )(x, o_a, o_b, mod3, g_post_mix, g_pre_ffn, g_post_ffn, w_out, w_gu, w_down)


def _in_perm_indices():
    splits = (NSA_HEADS * HEAD_DIM,) + (NSA_KV_HEADS * HEAD_DIM,) * 6 + (
        3 * NSA_HEADS, DSA_HEADS * HEAD_DIM, KV_LORA, ROPE_DIM, IDX_HEADS * IDX_DIM, IDX_DIM, IDX_HEADS)
    off = np.concatenate([[0], np.cumsum(splits)])
    seg = lambda i: np.arange(off[i], off[i + 1])
    (nq, nkc, nvc, nks, nvs, nkw, nvw, ngate, dq, dckv, dkr, iq, ik, iw) = [seg(i) for i in range(14)]
    return np.concatenate([nq, nkc, nvc, nks, nvs, nkw, nvw, dq, iq, dckv, ik, dkr, ngate, iw])


def _rope_tables(s):
    inv_freq = 1.0 / (ROPE_THETA ** (jnp.arange(0, ROPE_DIM, 2, dtype=F32) / ROPE_DIM))
    ang = jnp.arange(s, dtype=F32)[:, None] * inv_freq[None, :]
    cos, sin = jnp.cos(ang), jnp.sin(ang)
    ones = jnp.ones((s, HEAD_DIM - ROPE_DIM), F32)
    zeros8 = jnp.zeros((s, ROPE_HALF), F32)
    zeros = jnp.zeros((s, HEAD_DIM - ROPE_DIM), F32)
    cos_h = jnp.concatenate([cos, cos, ones], axis=1)
    sa_h = jnp.concatenate([-sin, zeros8, zeros], axis=1)
    sb_h = jnp.concatenate([zeros8, sin, zeros], axis=1)
    two = lambda t: jnp.concatenate([t, t], axis=1)
    return two(cos_h), two(sa_h), two(sb_h)


def _blockdiag2(a, b):
    za = jnp.zeros((a.shape[0], b.shape[1]), a.dtype)
    zb = jnp.zeros((b.shape[0], a.shape[1]), a.dtype)
    return jnp.concatenate([jnp.concatenate([a, za], axis=1), jnp.concatenate([zb, b], axis=1)], axis=0)


def _layer_weights(w_in, cmp_pe, cmp_w1, cmp_b1, cmp_w2, cmp_b2, w_uk, w_uv, s):
    d = w_in.shape[0]
    perm = _in_perm_indices()
    w_perm = jnp.concatenate([w_in[:, perm], jnp.zeros((d, IN_PERM_WIDTH - perm.size), w_in.dtype)], axis=1)
    w1 = cmp_w1.reshape(2, CMP_LEN, HEAD_DIM, CMP_HIDDEN)
    bd = jax.vmap(jax.vmap(lambda m: _blockdiag2(m, m)))
    w1a, w1b = bd(w1[:, :CMP_STRIDE]), bd(w1[:, CMP_STRIDE:])
    pe2 = jnp.concatenate([cmp_pe, cmp_pe], axis=-1)[:, :, None, :]
    pea, peb = pe2[:, :CMP_STRIDE], pe2[:, CMP_STRIDE:]
    b1 = jnp.concatenate([cmp_b1, cmp_b1], axis=-1)[:, None, :]
    w2 = jax.vmap(lambda m: _blockdiag2(m, m))(cmp_w2)
    b2 = jnp.concatenate([cmp_b2, cmp_b2], axis=-1)[:, None, :]
    def head_q(uk):
        top = jnp.concatenate([jnp.zeros((ROPE_DIM, KV_LORA), F32), jnp.eye(ROPE_DIM, dtype=F32),
                               jnp.zeros((ROPE_DIM, LANES - ROPE_DIM), F32)], axis=1)
        bot = jnp.concatenate([uk.T, jnp.zeros((NOPE_DIM, LANES), F32)], axis=1)
        return jnp.concatenate([top, bot], axis=0)
    hq = jax.vmap(head_q)(w_uk)
    wq = jax.vmap(_blockdiag2)(hq[0::2], hq[1::2])
    wuv = jax.vmap(_blockdiag2)(w_uv[0::2], w_uv[1::2])
    c = lambda a: a.astype(MXU_DTYPE)
    return dict(w_perm=c(w_perm), w1a=c(w1a), w1b=c(w1b), pea=pea, peb=peb, b1=b1, w2=c(w2), b2=b2,
                wq=c(wq), wuv=c(wuv))


def _selection_constants(s):
    n_cmp = (s - CMP_LEN) // CMP_STRIDE + 1
    n_slc = s // SLC_BLOCK
    ci = np.arange(n_cmp)[:, None] * CMP_STRIDE
    sj = np.arange(n_slc)[None, :] * SLC_BLOCK
    ov = np.zeros((s // CMP_STRIDE, LANES), np.float32)
    ov[:n_cmp, :n_slc] = ((ci < sj + SLC_BLOCK) & (ci + CMP_LEN > sj)).astype(np.float32)
    eful = np.zeros((LANES, s), np.float32)
    eful[np.arange(s) // SLC_BLOCK, np.arange(s)] = 1.0
    tri = np.triu(np.ones((TK, TK), np.float32))
    c = lambda a: jnp.asarray(a).astype(MXU_DTYPE)
    return c(ov), c(eful), c(tri)


def kernel(x, c, w_ada, b_ada, g_pre_mix, g_post_mix, g_pre_ffn, g_post_ffn, w_in, cmp_pe, cmp_w1, cmp_b1,
           cmp_w2, cmp_b2, g_kv_norm, w_uk, w_uv, w_out, w_gate_up, w_down):
    bsz, s, d = x.shape
    depth = w_ada.shape[0]
    assert s % TM_PROJ == 0 and s // CMP_STRIDE == LANES and s // SLC_BLOCK <= LANES
    cos_t, sa_t, sb_t = _rope_tables(s)
    ov, eful, tri = _selection_constants(s)
    for l in range(depth):
        lw = _layer_weights(w_in[l], cmp_pe[l], cmp_w1[l], cmp_b1[l], cmp_w2[l], cmp_b2[l], w_uk[l], w_uv[l], s)
        mod3 = _ada_call(c, w_ada[l], b_ada[l]).reshape(bsz, 6, d)
        nq, ncv, nkv, dq, iq, kcat, kidx, misc = _inproj_call(
            x, mod3, g_pre_mix[l][None], g_kv_norm[l][None], lw["w_perm"], cos_t, sa_t, sb_t)
        kcv = _compress_call(ncv, lw["w1a"], lw["w1b"], lw["pea"], lw["peb"], lw["b1"], lw["w2"], lw["b2"])
        o_a = _nsa_call(nq, nkv, kcv, misc, ov, eful)
        o_b = _dsa_call(dq, iq, misc, kidx, kcat, lw["wq"], lw["wuv"], tri)
        x = _ffn_call(x, o_a, o_b, mod3, g_post_mix[l][None], g_pre_ffn[l][None], g_post_ffn[l][None],
                      w_out[l].astype(MXU_DTYPE), w_gate_up[l].astype(MXU_DTYPE), w_down[l].astype(MXU_DTYPE))
    return x
```

```python
import functools

import numpy as np
import jax
import jax.numpy as jnp
from jax import lax
from jax.experimental import pallas as pl
from jax.experimental.pallas import tpu as pltpu

F32 = jnp.float32
I32 = jnp.int32
MXU_DTYPE = jnp.bfloat16

HEAD_DIM = 64
ROPE_DIM = HEAD_DIM // 4
ROPE_HALF = ROPE_DIM // 2
ROPE_THETA = 500000.0
EPS = 1e-6
NEG_INF = -1e30
FORCED_SCORE = 1e6
NSA_HEADS = 8
NSA_KV_HEADS = 2
NSA_GROUP = NSA_HEADS // NSA_KV_HEADS
CMP_LEN = 32
CMP_STRIDE = 16
CMP_HIDDEN = 128
SLC_BLOCK = 64
SLC_TOPN = 8
WINDOW = 512
DSA_HEADS = 8
KV_LORA = 128
NOPE_DIM = HEAD_DIM - ROPE_DIM
IDX_HEADS = 8
IDX_DIM = 64
IDX_TOPK_MAX = 256

LANES = 128
TQ = 256
TC = 512
WIN_TC = 256
COUNT_ROWS = 32
TM_PROJ = 512
TM_FFN = 512
FF_CHUNK = 256
INT_MIN = -(2 ** 31)
VMEM_LIMIT = 56 * 1024 * 1024

MISC_IK = 0
MISC_KR = 64
MISC_GATE = 80
MISC_IW = 104
ONES_LANE = 2 * LANES - 1


def _dot(a, b):
    return jnp.dot(a, b, preferred_element_type=F32)


def _dot_nt(a, b):
    return lax.dot_general(a, b, (((1,), (1,)), ((), ())), preferred_element_type=F32)


def _sigmoid(v):
    return 1.0 / (1.0 + jnp.exp(-v))


def _roll_lanes(v, shift):
    return pltpu.roll(v, shift, 1)


def _swap_halves(v):
    return _roll_lanes(v.astype(F32), LANES // 2).astype(v.dtype)


def _rep(v, n):
    return v if n == 1 else jnp.concatenate([v] * n, axis=1)


def _ada_kernel(c_ref, w_ref, b_ref, o_ref):
    c = c_ref[...]
    act = (c * _sigmoid(c)).astype(MXU_DTYPE)
    o_ref[...] = _dot(act, w_ref[...].astype(MXU_DTYPE)) + b_ref[...]


def _ada_call(c, w_ada, b_ada):
    bsz, d = c.shape
    n = w_ada.shape[1]
    tn = d
    return pl.pallas_call(
        _ada_kernel,
        grid=(n // tn,),
        in_specs=[pl.BlockSpec((bsz, d), lambda j: (0, 0)),
                  pl.BlockSpec((d, tn), lambda j: (0, j)),
                  pl.BlockSpec((1, tn), lambda j: (0, j))],
        out_specs=pl.BlockSpec((bsz, tn), lambda j: (0, j)),
        out_shape=jax.ShapeDtypeStruct((bsz, n), F32),
        name="adaln_mod",
    )(c, w_ada, b_ada.reshape(1, n))


G_NQ = (0, 512)
G_NCV = (512, 768)
G_NKV = (768, 1280)
G_DQ = (1280, 1792)
G_IQ = (1792, 2304)
G_CKV = (2304, 2432)
G_MISC = (2432, 2560)
IN_PERM_WIDTH = 2560


def _rope_tile(v, cos, sin_a, sin_b):
    return v * cos + _roll_lanes(v, LANES - ROPE_HALF) * sin_a + _roll_lanes(v, ROPE_HALF) * sin_b


def _inproj_kernel(x_ref, mod_ref, g_ref, gkv_ref, w_ref, cos_ref, sa_ref, sb_ref,
                   nq_ref, ncv_ref, nkv_ref, dq_ref, iq_ref, kcat_ref, kidx_ref, misc_ref):
    x = x_ref[0]
    y = x * lax.rsqrt(jnp.mean(x * x, axis=-1, keepdims=True) + EPS) * g_ref[...]
    shift = mod_ref[0, 0:1, :]
    scale = mod_ref[0, 1:2, :]
    h = (y * (1.0 + scale) + shift).astype(MXU_DTYPE)
    cos, sin_a, sin_b = cos_ref[...], sa_ref[...], sb_ref[...]
    qscale = HEAD_DIM ** -0.5

    def proj(lo, hi):
        return _dot(h, w_ref[:, lo:hi])

    def rope_cols(v, roped):
        tiles = []
        for j in range(v.shape[1] // LANES):
            t = v[:, j * LANES:(j + 1) * LANES]
            tiles.append(_rope_tile(t, cos, sin_a, sin_b) if roped[j] else t)
        return tiles

    for j, t in enumerate(rope_cols(proj(*G_NQ), [True] * 4)):
        nq_ref[0, :, j * LANES:(j + 1) * LANES] = (t * qscale).astype(nq_ref.dtype)
    for j, t in enumerate(rope_cols(proj(*G_NCV), [True, False])):
        ncv_ref[0, :, j * LANES:(j + 1) * LANES] = t
    for j, t in enumerate(rope_cols(proj(*G_NKV), [True, False, True, False])):
        nkv_ref[0, :, j * LANES:(j + 1) * LANES] = t.astype(nkv_ref.dtype)
    for j, t in enumerate(rope_cols(proj(*G_DQ), [True] * 4)):
        dq_ref[0, :, j * LANES:(j + 1) * LANES] = (t * qscale).astype(dq_ref.dtype)
    for j, t in enumerate(rope_cols(proj(*G_IQ), [True] * 4)):
        iq_ref[0, :, j * LANES:(j + 1) * LANES] = t.astype(iq_ref.dtype)

    ckv = proj(*G_CKV)
    ckv = ckv * lax.rsqrt(jnp.mean(ckv * ckv, axis=-1, keepdims=True) + EPS) * gkv_ref[...]
    misc = _rope_tile(proj(*G_MISC), cos, sin_a, sin_b)
    misc_ref[0] = misc
    lane = lax.broadcasted_iota(I32, misc.shape, 1)
    kcat_ref[0, :, 0:LANES] = ckv.astype(kcat_ref.dtype)
    kr = jnp.where(lane < ROPE_DIM, _roll_lanes(misc, LANES - MISC_KR),
                   jnp.where(lane == ONES_LANE - LANES, 1.0, 0.0))
    kcat_ref[0, :, LANES:2 * LANES] = kr.astype(kcat_ref.dtype)
    ik = jnp.where(lane < IDX_DIM, misc, 0.0)
    kidx_ref[0, :, 0:LANES] = ik.astype(kidx_ref.dtype)
    kidx_ref[0, :, LANES:2 * LANES] = _roll_lanes(ik, LANES // 2).astype(kidx_ref.dtype)


def _inproj_call(x, mod3, g_pre, g_kv, w_perm, cos_t, sa_t, sb_t):
    bsz, s, d = x.shape
    tm = min(TM_PROJ, s)
    row = lambda width: pl.BlockSpec((1, tm, width), lambda b, i: (b, i, 0))
    const2 = lambda shape: pl.BlockSpec(shape, lambda b, i: (0, 0))
    tab = pl.BlockSpec((tm, LANES), lambda b, i: (i, 0))
    widths = [(512, MXU_DTYPE), (256, F32), (512, MXU_DTYPE), (512, MXU_DTYPE), (512, MXU_DTYPE),
              (256, MXU_DTYPE), (256, MXU_DTYPE), (LANES, F32)]
    return pl.pallas_call(
        _inproj_kernel,
        grid=(bsz, s // tm),
        in_specs=[row(d),
                  pl.BlockSpec((1, 6, d), lambda b, i: (b, 0, 0)),
                  const2((1, d)), const2((1, KV_LORA)), const2((d, IN_PERM_WIDTH)),
                  tab, tab, tab],
        out_specs=[row(w) for w, _ in widths],
        out_shape=[jax.ShapeDtypeStruct((bsz, s, w), dt) for w, dt in widths],
        compiler_params=pltpu.CompilerParams(vmem_limit_bytes=VMEM_LIMIT),
        name="inproj_rope",
    )(x, mod3, g_pre, g_kv, w_perm, cos_t, sa_t, sb_t)


def _gelu_tanh(v):
    return 0.5 * v * (1.0 + jnp.tanh(np.sqrt(2.0 / np.pi) * (v + 0.044715 * (v * v * v))))


def _compress_kernel(x_ref, w1a_ref, w1b_ref, pea_ref, peb_ref, b1_ref, w2_ref, b2_ref, o_ref):
    n_chunks = x_ref.shape[1]
    for kv in range(2):
        first = jnp.zeros((n_chunks, 2 * CMP_HIDDEN), F32)
        second = jnp.zeros((n_chunks, 2 * CMP_HIDDEN), F32)
        for l in range(CMP_STRIDE):
            lo = l * 2 * LANES + kv * LANES
            slab = x_ref[0, :, lo:lo + LANES]
            first = first + _dot((slab + pea_ref[kv, l]).astype(MXU_DTYPE), w1a_ref[kv, l])
            second = second + _dot((slab + peb_ref[kv, l]).astype(MXU_DTYPE), w1b_ref[kv, l])
        hid = first + pltpu.roll(second, n_chunks - 1, 0) + b1_ref[kv]
        out = _dot(_gelu_tanh(hid).astype(MXU_DTYPE), w2_ref[kv]) + b2_ref[kv]
        o_ref[0, :, kv * LANES:(kv + 1) * LANES] = out.astype(o_ref.dtype)


def _compress_call(ncv, w1a, w1b, pea, peb, b1, w2, b2):
    bsz, s, _ = ncv.shape
    n_chunks = s // CMP_STRIDE
    xr = ncv.reshape(bsz, n_chunks, CMP_STRIDE * 2 * LANES)
    full = lambda a: pl.BlockSpec(a.shape, lambda b: (0,) * a.ndim)
    return pl.pallas_call(
        _compress_kernel,
        grid=(bsz,),
        in_specs=[pl.BlockSpec((1, n_chunks, CMP_STRIDE * 2 * LANES), lambda b: (b, 0, 0)),
                  full(w1a), full(w1b), full(pea), full(peb), full(b1), full(w2), full(b2)],
        out_specs=pl.BlockSpec((1, n_chunks, 2 * LANES), lambda b: (b, 0, 0)),
        out_shape=jax.ShapeDtypeStruct((bsz, n_chunks, 2 * LANES), MXU_DTYPE),
        compiler_params=pltpu.CompilerParams(vmem_limit_bytes=VMEM_LIMIT),
        name="nsa_compress",
    )(xr, w1a, w1b, pea, peb, b1, w2, b2)


def _split3(v):
    hi = v.astype(MXU_DTYPE)
    r1 = v - hi.astype(F32)
    mid = r1.astype(MXU_DTYPE)
    lo = (r1 - mid.astype(F32)).astype(MXU_DTYPE)
    return hi, mid, lo


def _attn_init(m_ref, l_ref, acc_ref):
    m_ref[...] = jnp.full(m_ref.shape, NEG_INF, F32)
    acc_ref[...] = jnp.zeros(acc_ref.shape, F32)
    if l_ref is not None:
        l_ref[...] = jnp.zeros(l_ref.shape, F32)


def _attn_chunk(qs_ref, k, v, bias_of_head, tc, s_ref, e_ref, m_ref, l_ref, alpha_ref, acc_ref):
    nh = m_ref.shape[0]
    n_lt = tc // LANES
    s_ref[:, 0:tc] = _dot_nt(qs_ref[...], k)
    for h in range(nh):
        r0 = h * TQ
        sh = s_ref[r0:r0 + TQ, 0:tc] + bias_of_head(h)
        m_old = m_ref[h]
        m_new = jnp.maximum(m_old, jnp.max(sh, axis=-1, keepdims=True))
        alpha = jnp.exp(m_old - m_new)
        m_ref[h] = m_new
        alpha_ref[h] = alpha
        row_sum = None
        for j in range(n_lt):
            e = jnp.exp(sh[:, j * LANES:(j + 1) * LANES] - m_new)
            e_ref[r0:r0 + TQ, j * LANES:(j + 1) * LANES] = e.astype(e_ref.dtype)
            if l_ref is not None:
                row_sum = e if row_sum is None else row_sum + e
        if l_ref is not None:
            l_ref[h] = alpha * l_ref[h] + jnp.sum(row_sum, axis=-1, keepdims=True)
    pv = _dot(e_ref[:, 0:tc], v).reshape(acc_ref.shape)
    for j in range(acc_ref.shape[2] // LANES):
        sl = slice(j * LANES, (j + 1) * LANES)
        acc_ref[:, :, sl] = alpha_ref[...] * acc_ref[:, :, sl] + pv[:, :, sl]


def _nsa_kernel(q_ref, kv_ref, kcv_ref, misc_ref, ovt_ref, eful_ref, o_ref,
                qs_ref, bias_ref, oc_ref, os_ref, s_ref, e_ref, m_ref, l_ref, alpha_ref, acc_ref):
    qb = pl.program_id(1)
    t0 = qb * TQ
    n_ch_max = bias_ref.shape[1]
    n_slc = n_ch_max * TC // SLC_BLOCK
    n_ch = (t0 + TQ + TC - 1) // TC
    half = LANES // 2
    row = lax.broadcasted_iota(I32, (TQ, LANES), 0)
    lane = lax.broadcasted_iota(I32, (TQ, LANES), 1)
    t_abs = t0 + row

    for hd in range(NSA_HEADS):
        g = hd // NSA_GROUP
        t = q_ref[0, :, (hd // 2) * LANES:(hd // 2 + 1) * LANES]
        if hd % 2 != g:
            t = _swap_halves(t)
        in_g = (lane >= half * g) & (lane < half * (g + 1))
        qs_ref[hd * TQ:(hd + 1) * TQ, :] = jnp.where(in_g, t, jnp.zeros_like(t))

    kc = kcv_ref[0, :, 0:LANES]
    vc = kcv_ref[0, :, LANES:2 * LANES]
    mask_c = (lane * CMP_STRIDE + (CMP_LEN - 1)) <= t_abs
    s3 = jnp.where(mask_c[None], _dot_nt(qs_ref[...], kc).reshape(NSA_HEADS, TQ, LANES), NEG_INF)
    mx = jnp.max(s3, axis=-1, keepdims=True)
    e_c = jnp.where(mask_c[None], jnp.exp(s3 - mx), 0.0)
    p_c = e_c / jnp.maximum(jnp.sum(e_c, axis=-1, keepdims=True), 1e-30)
    oc_ref[...] = _dot(p_c.reshape(NSA_HEADS * TQ, LANES).astype(MXU_DTYPE), vc).reshape(oc_ref.shape)

    blk_row = lax.broadcasted_iota(I32, (LANES, TQ), 0)
    blk_t = (t0 + lax.broadcasted_iota(I32, (LANES, TQ), 1)) // SLC_BLOCK
    visible = blk_row <= blk_t
    forced = (blk_row == 0) | (blk_row == blk_t) | (blk_row == blk_t - 1)
    for g in range(NSA_KV_HEADS):
        psum = p_c[g * NSA_GROUP]
        for hh in range(1, NSA_GROUP):
            psum = psum + p_c[g * NSA_GROUP + hh]
        ovt = ovt_ref[...]
        imp = sum(_dot_nt(ovt, piece) for piece in _split3(psum))
        imp = jnp.where(visible, jnp.where(forced, FORCED_SCORE, imp), -jnp.inf)
        rank = jnp.zeros((LANES, TQ), I32)
        for jp in range(n_slc):
            other = imp[jp:jp + 1, :]
            beats = (other > imp) | ((other == imp) & (blk_row > jp))
            rank = rank + jnp.where(beats, 1, 0)
        sel_t = jnp.where((rank < min(SLC_TOPN, n_slc)) & (blk_row < n_slc), 1.0, 0.0)
        sel = sel_t.T.astype(MXU_DTYPE)
        selexp = _dot(sel, eful_ref[...])
        for c in range(n_ch_max):
            for j in range(TC // LANES):
                kpos = c * TC + j * LANES + lane
                ok = (selexp[:, c * TC + j * LANES:c * TC + (j + 1) * LANES] > 0.5) & (kpos <= t_abs)
                bias_ref[g, c, :, j * LANES:(j + 1) * LANES] = jnp.where(ok, 0.0, NEG_INF)

    def sel_body(c, carry):
        start = pl.multiple_of(c * TC, TC)
        k = kv_ref[0, pl.ds(start, TC), 0:LANES]
        v = kv_ref[0, pl.ds(start, TC), LANES:2 * LANES]
        _attn_chunk(qs_ref, k, v, lambda h: bias_ref[h // NSA_GROUP, c], TC,
                    s_ref, e_ref, m_ref, l_ref, alpha_ref, acc_ref)
        return carry

    _attn_init(m_ref, l_ref, acc_ref)
    lax.fori_loop(0, n_ch, sel_body, 0)
    os_ref[...] = acc_ref[...] / jnp.maximum(l_ref[...], 1e-30)

    def win_body(c, carry):
        start = pl.multiple_of(c * WIN_TC, WIN_TC)
        k = kv_ref[0, pl.ds(start, WIN_TC), 2 * LANES:3 * LANES]
        v = kv_ref[0, pl.ds(start, WIN_TC), 3 * LANES:4 * LANES]
        tiles = []
        for j in range(WIN_TC // LANES):
            kpos = start + j * LANES + lane
            tiles.append(jnp.where((kpos <= t_abs) & (kpos > t_abs - WINDOW), 0.0, NEG_INF))
        bias = jnp.concatenate(tiles, axis=1)
        _attn_chunk(qs_ref, k, v, lambda h: bias, WIN_TC, s_ref, e_ref, m_ref, l_ref, alpha_ref, acc_ref)
        return carry

    _attn_init(m_ref, l_ref, acc_ref)
    lax.fori_loop(jnp.maximum(qb - WINDOW // WIN_TC, 0), qb + 1, win_body, 0)

    gates = _sigmoid(misc_ref[0])
    heads = []
    for hd in range(NSA_HEADS):
        gc = gates[:, MISC_GATE + hd:MISC_GATE + hd + 1]
        gs = gates[:, MISC_GATE + NSA_HEADS + hd:MISC_GATE + NSA_HEADS + hd + 1]
        gw = gates[:, MISC_GATE + 2 * NSA_HEADS + hd:MISC_GATE + 2 * NSA_HEADS + hd + 1]
        o_w = acc_ref[hd] / jnp.maximum(l_ref[hd], 1e-30)
        heads.append(gc * oc_ref[hd] + gs * os_ref[hd] + gw * o_w)
    for pair in range(NSA_HEADS // 2):
        even, odd = heads[2 * pair], heads[2 * pair + 1]
        if (2 * pair) // NSA_GROUP == 0:
            odd = _roll_lanes(odd, half)
        else:
            even = _roll_lanes(even, half)
        o_ref[0, :, pair * LANES:(pair + 1) * LANES] = jnp.where(lane < half, even, odd).astype(o_ref.dtype)


def _nsa_call(nq, nkv, kcv, misc, ovt, eful):
    bsz, s, _ = nq.shape
    n_ch = s // TC
    rowspec = lambda width: pl.BlockSpec((1, TQ, width), lambda b, i: (b, i, 0))
    batch = lambda a: pl.BlockSpec((1,) + a.shape[1:], lambda b, i: (b, 0, 0))
    const = lambda a: pl.BlockSpec(a.shape, lambda b, i: (0, 0))
    stat = pltpu.VMEM((NSA_HEADS, TQ, LANES), F32)
    return pl.pallas_call(
        _nsa_kernel,
        grid=(bsz, s // TQ),
        in_specs=[rowspec(512), batch(nkv), batch(kcv), rowspec(LANES), const(ovt), const(eful)],
        out_specs=rowspec(512),
        out_shape=jax.ShapeDtypeStruct((bsz, s, 512), MXU_DTYPE),
        scratch_shapes=[pltpu.VMEM((NSA_HEADS * TQ, LANES), MXU_DTYPE),
                        pltpu.VMEM((NSA_KV_HEADS, n_ch, TQ, TC), F32),
                        stat, stat,
                        pltpu.VMEM((NSA_HEADS * TQ, TC), F32),
                        pltpu.VMEM((NSA_HEADS * TQ, TC), MXU_DTYPE),
                        stat, stat, stat, stat],
        compiler_params=pltpu.CompilerParams(vmem_limit_bytes=VMEM_LIMIT),
        name="nsa_attention",
    )(nq, nkv, kcv, misc, ovt, eful)


def _dsa_kernel(dq_ref, iq_ref, misc_ref, kidx_ref, kcat_ref, wq_ref, wuv_ref, tri_ref, ones_ref, o_ref,
                keys_ref, keys_t_ref, bias_ref, qcat_ref, iqs_ref, wbc_ref, s_ref, e_ref, m_ref, alpha_ref, acc_ref,
                *, k_sel):
    qb = pl.program_id(1)
    t0 = qb * TQ
    n_ch = (t0 + TQ + TC - 1) // TC
    n_lt = TC // LANES
    row = lax.broadcasted_iota(I32, (TQ, LANES), 0)
    lane = lax.broadcasted_iota(I32, (TQ, LANES), 1)
    t_abs = t0 + row

    for pair in range(DSA_HEADS // 2):
        cols = slice(pair * LANES, (pair + 1) * LANES)
        qc = _dot(dq_ref[0, :, cols], wq_ref[pair]).astype(MXU_DTYPE)
        qcat_ref[(2 * pair) * TQ:(2 * pair + 1) * TQ, :] = qc[:, 0:2 * LANES]
        qcat_ref[(2 * pair + 1) * TQ:(2 * pair + 2) * TQ, :] = qc[:, 2 * LANES:4 * LANES]
        iqs_ref[pair * TQ:(pair + 1) * TQ, :] = iq_ref[0, :, cols]
    w_idx = misc_ref[0] * (IDX_HEADS ** -0.5 * IDX_DIM ** -0.5)
    for hd in range(IDX_HEADS):
        wbc_ref[hd] = jnp.broadcast_to(w_idx[:, MISC_IW + hd:MISC_IW + hd + 1], (TQ, LANES))

    def idx_body(c, carry):
        start = pl.multiple_of(c * TC, TC)
        lg_even = _dot_nt(iqs_ref[...], kidx_ref[0, pl.ds(start, TC), 0:LANES])
        lg_odd = _dot_nt(iqs_ref[...], kidx_ref[0, pl.ds(start, TC), LANES:2 * LANES])
        for j in range(n_lt):
            cols = slice(j * LANES, (j + 1) * LANES)
            score = jnp.zeros((TQ, LANES), F32)
            for pair in range(IDX_HEADS // 2):
                rows = slice(pair * TQ, (pair + 1) * TQ)
                score = score + wbc_ref[2 * pair] * jnp.maximum(lg_even[rows, cols], 0.0)
                score = score + wbc_ref[2 * pair + 1] * jnp.maximum(lg_odd[rows, cols], 0.0)
            bits = lax.bitcast_convert_type(score, I32)
            key = bits ^ ((bits >> 31) & 0x7FFFFFFF)
            key = jnp.where(score == 0.0, 0, key)
            keys_ref[c, :, cols] = jnp.where(start + j * LANES + lane <= t_abs, key, INT_MIN)
        keys_t_ref[c] = keys_ref[c].T
        return carry

    lax.fori_loop(0, n_ch, idx_body, 0)

    def count(pred):
        def body(c, part):
            hit = jnp.where(pred(keys_t_ref[c]), 1.0, 0.0)
            return part + jnp.sum(hit.reshape(TC // COUNT_ROWS, COUNT_ROWS, TQ), axis=0)
        part = lax.fori_loop(0, n_ch, body, jnp.zeros((COUNT_ROWS, TQ), F32))
        return jnp.sum(part, axis=0, keepdims=True)

    kf = float(k_sel)
    zero = jnp.zeros((1, TQ), I32)
    thr_t = jnp.where(count(lambda k: k >= zero) >= kf, zero, jnp.full((1, TQ), INT_MIN, I32))

    def bit_body(i, thr_t):
        cand = thr_t | (1 << (30 - i))
        return jnp.where(count(lambda k: k >= cand) >= kf, cand, thr_t)

    thr_t = lax.fori_loop(0, 31, bit_body, thr_t)
    thr_t = jnp.maximum(thr_t, INT_MIN + 1)
    need_t = kf - count(lambda k: k > thr_t)
    thr = jnp.broadcast_to(thr_t, (LANES, TQ)).T
    need = jnp.broadcast_to(need_t, (LANES, TQ)).T
    thr_w = _rep(thr, n_lt)
    need_w = _rep(need, n_lt)

    def bias_body(c, run):
        key = keys_ref[c]
        eq = key == thr_w
        eqf = jnp.where(eq, 1.0, 0.0).astype(MXU_DTYPE)
        prefix = _dot(eqf, tri_ref[...])
        take = (key > thr_w) | (eq & (_rep(run, n_lt) + prefix <= need_w))
        bias_ref[c] = jnp.where(take, 0.0, NEG_INF)
        return run + _dot(eqf, ones_ref[...])

    lax.fori_loop(0, n_ch, bias_body, jnp.zeros((TQ, LANES), F32))

    def att_body(c, carry):
        start = pl.multiple_of(c * TC, TC)
        kc = kcat_ref[0, pl.ds(start, TC), :]
        _attn_chunk(qcat_ref, kc, kc, lambda h: bias_ref[c], TC,
                    s_ref, e_ref, m_ref, None, alpha_ref, acc_ref)
        return carry

    _attn_init(m_ref, None, acc_ref)
    lax.fori_loop(0, n_ch, att_body, 0)

    for pair in range(DSA_HEADS // 2):
        outs = []
        for hd in (2 * pair, 2 * pair + 1):
            a = acc_ref[hd]
            outs.append(a[:, 0:LANES] / jnp.maximum(a[:, ONES_LANE:ONES_LANE + 1], 1e-30))
        both = jnp.concatenate(outs, axis=1).astype(MXU_DTYPE)
        o_ref[0, :, pair * LANES:(pair + 1) * LANES] = _dot(both, wuv_ref[pair]).astype(o_ref.dtype)


def _dsa_call(dq, iq, misc, kidx, kcat, wq, wuv, tri, ones):
    bsz, s, _ = dq.shape
    n_ch = s // TC
    k_sel = min(IDX_TOPK_MAX, s // 4)
    rowspec = lambda width: pl.BlockSpec((1, TQ, width), lambda b, i: (b, i, 0))
    batch = lambda a: pl.BlockSpec((1,) + a.shape[1:], lambda b, i: (b, 0, 0))
    const = lambda a: pl.BlockSpec(a.shape, lambda b, i: (0,) * a.ndim)
    stat = pltpu.VMEM((DSA_HEADS, TQ, LANES), F32)
    return pl.pallas_call(
        functools.partial(_dsa_kernel, k_sel=k_sel),
        grid=(bsz, s // TQ),
        in_specs=[rowspec(512), rowspec(512), rowspec(LANES), batch(kidx), batch(kcat),
                  const(wq), const(wuv), const(tri), const(ones)],
        out_specs=rowspec(512),
        out_shape=jax.ShapeDtypeStruct((bsz, s, 512), MXU_DTYPE),
        scratch_shapes=[pltpu.VMEM((n_ch, TQ, TC), I32),
                        pltpu.VMEM((n_ch, TC, TQ), I32),
                        pltpu.VMEM((n_ch, TQ, TC), F32),
                        pltpu.VMEM((DSA_HEADS * TQ, 2 * LANES), MXU_DTYPE),
                        pltpu.VMEM((IDX_HEADS // 2 * TQ, LANES), MXU_DTYPE),
                        stat,
                        pltpu.VMEM((DSA_HEADS * TQ, TC), F32),
                        pltpu.VMEM((DSA_HEADS * TQ, TC), MXU_DTYPE),
                        stat, stat,
                        pltpu.VMEM((DSA_HEADS, TQ, 2 * LANES), F32)],
        compiler_params=pltpu.CompilerParams(vmem_limit_bytes=VMEM_LIMIT),
        name="dsa_attention",
    )(dq, iq, misc, kidx, kcat, wq, wuv, tri, ones)


def _rms(v, g):
    return v * lax.rsqrt(jnp.mean(v * v, axis=-1, keepdims=True) + EPS) * g


def _ffn_kernel(x_ref, oa_ref, ob_ref, mod_ref, gpm_ref, gpf_ref, gqf_ref, wout_ref, wgu_ref, wd_ref,
                o_ref, *, d_ff):
    x = x_ref[0]
    half = oa_ref.shape[2]
    y = _dot(oa_ref[0], wout_ref[0:half, :]) + _dot(ob_ref[0], wout_ref[half:2 * half, :])
    x1 = x + mod_ref[0, 2:3, :] * _rms(y, gpm_ref[...])
    h = (_rms(x1, gpf_ref[...]) * (1.0 + mod_ref[0, 4:5, :]) + mod_ref[0, 3:4, :]).astype(MXU_DTYPE)
    acc = jnp.zeros(x.shape, F32)
    for c in range(d_ff // FF_CHUNK):
        lo = c * FF_CHUNK
        gate = _dot(h, wgu_ref[:, lo:lo + FF_CHUNK])
        up = _dot(h, wgu_ref[:, d_ff + lo:d_ff + lo + FF_CHUNK])
        act = (gate * _sigmoid(gate) * up).astype(MXU_DTYPE)
        acc = acc + _dot(act, wd_ref[lo:lo + FF_CHUNK, :])
    o_ref[0] = x1 + mod_ref[0, 5:6, :] * _rms(acc, gqf_ref[...])


def _ffn_call(x, o_a, o_b, mod3, g_post_mix, g_pre_ffn, g_post_ffn, w_out, w_gu, w_down):
    bsz, s, d = x.shape
    d_ff = w_down.shape[0]
    tm = min(TM_FFN, s)
    row = lambda width: pl.BlockSpec((1, tm, width), lambda b, i: (b, i, 0))
    const = lambda a: pl.BlockSpec(a.shape, lambda b, i: (0, 0), pipeline_mode=pl.Buffered(1))
    return pl.pallas_call(
        functools.partial(_ffn_kernel, d_ff=d_ff),
        grid=(bsz, s // tm),
        in_specs=[row(d), row(o_a.shape[2]), row(o_b.shape[2]),
                  pl.BlockSpec((1, 6, d), lambda b, i: (b, 0, 0)),
                  const(g_post_mix), const(g_pre_ffn), const(g_post_ffn),
                  const(w_out), const(w_gu), const(w_down)],
        out_specs=row(d),
        out_shape=jax.ShapeDtypeStruct((bsz, s, d), x.dtype),
        compiler_params=pltpu.CompilerParams(vmem_limit_bytes=VMEM_LIMIT),
        name="outproj_ffn",
    )(x, o_a, o_b, mod3, g_post_mix, g_pre_ffn, g_post_ffn, w_out, w_gu, w_down)


def _in_perm_indices():
    splits = (NSA_HEADS * HEAD_DIM,) + (NSA_KV_HEADS * HEAD_DIM,) * 6 + (
        3 * NSA_HEADS, DSA_HEADS * HEAD_DIM, KV_LORA, ROPE_DIM, IDX_HEADS * IDX_DIM, IDX_DIM, IDX_HEADS)
    off = np.concatenate([[0], np.cumsum(splits)])
    seg = lambda i: np.arange(off[i], off[i + 1])
    (nq, nkc, nvc, nks, nvs, nkw, nvw, ngate, dq, dckv, dkr, iq, ik, iw) = [seg(i) for i in range(14)]
    return np.concatenate([nq, nkc, nvc, nks, nvs, nkw, nvw, dq, iq, dckv, ik, dkr, ngate, iw])


def _rope_tables(s):
    inv_freq = 1.0 / (ROPE_THETA ** (jnp.arange(0, ROPE_DIM, 2, dtype=F32) / ROPE_DIM))
    ang = jnp.arange(s, dtype=F32)[:, None] * inv_freq[None, :]
    cos, sin = jnp.cos(ang), jnp.sin(ang)
    ones = jnp.ones((s, HEAD_DIM - ROPE_DIM), F32)
    zeros8 = jnp.zeros((s, ROPE_HALF), F32)
    zeros = jnp.zeros((s, HEAD_DIM - ROPE_DIM), F32)
    cos_h = jnp.concatenate([cos, cos, ones], axis=1)
    sa_h = jnp.concatenate([-sin, zeros8, zeros], axis=1)
    sb_h = jnp.concatenate([zeros8, sin, zeros], axis=1)
    two = lambda t: jnp.concatenate([t, t], axis=1)
    return two(cos_h), two(sa_h), two(sb_h)


def _blockdiag2(a, b):
    za = jnp.zeros((a.shape[0], b.shape[1]), a.dtype)
    zb = jnp.zeros((b.shape[0], a.shape[1]), a.dtype)
    return jnp.concatenate([jnp.concatenate([a, za], axis=1), jnp.concatenate([zb, b], axis=1)], axis=0)


def _layer_weights(w_in, cmp_pe, cmp_w1, cmp_b1, cmp_w2, cmp_b2, w_uk, w_uv, s):
    d = w_in.shape[0]
    perm = _in_perm_indices()
    w_perm = jnp.concatenate([w_in[:, perm], jnp.zeros((d, IN_PERM_WIDTH - perm.size), w_in.dtype)], axis=1)
    w1 = cmp_w1.reshape(2, CMP_LEN, HEAD_DIM, CMP_HIDDEN)
    bd = jax.vmap(jax.vmap(lambda m: _blockdiag2(m, m)))
    w1a, w1b = bd(w1[:, :CMP_STRIDE]), bd(w1[:, CMP_STRIDE:])
    pe2 = jnp.concatenate([cmp_pe, cmp_pe], axis=-1)[:, :, None, :]
    pea, peb = pe2[:, :CMP_STRIDE], pe2[:, CMP_STRIDE:]
    b1 = jnp.concatenate([cmp_b1, cmp_b1], axis=-1)[:, None, :]
    w2 = jax.vmap(lambda m: _blockdiag2(m, m))(cmp_w2)
    b2 = jnp.concatenate([cmp_b2, cmp_b2], axis=-1)[:, None, :]
    def head_q(uk):
        top = jnp.concatenate([jnp.zeros((ROPE_DIM, KV_LORA), F32), jnp.eye(ROPE_DIM, dtype=F32),
                               jnp.zeros((ROPE_DIM, LANES - ROPE_DIM), F32)], axis=1)
        bot = jnp.concatenate([uk.T, jnp.zeros((NOPE_DIM, LANES), F32)], axis=1)
        return jnp.concatenate([top, bot], axis=0)
    hq = jax.vmap(head_q)(w_uk)
    wq = jax.vmap(_blockdiag2)(hq[0::2], hq[1::2])
    wuv = jax.vmap(_blockdiag2)(w_uv[0::2], w_uv[1::2])
    c = lambda a: a.astype(MXU_DTYPE)
    return dict(w_perm=c(w_perm), w1a=c(w1a), w1b=c(w1b), pea=pea, peb=peb, b1=b1, w2=c(w2), b2=b2,
                wq=c(wq), wuv=c(wuv))


def _selection_constants(s):
    n_cmp = (s - CMP_LEN) // CMP_STRIDE + 1
    n_slc = s // SLC_BLOCK
    ci = np.arange(n_cmp)[:, None] * CMP_STRIDE
    sj = np.arange(n_slc)[None, :] * SLC_BLOCK
    ovt = np.zeros((LANES, s // CMP_STRIDE), np.float32)
    ovt[:n_slc, :n_cmp] = ((ci < sj + SLC_BLOCK) & (ci + CMP_LEN > sj)).astype(np.float32).T
    eful = np.zeros((LANES, s), np.float32)
    eful[np.arange(s) // SLC_BLOCK, np.arange(s)] = 1.0
    tri = np.triu(np.ones((TC, TC), np.float32))
    ones = np.ones((TC, LANES), np.float32)
    c = lambda a: jnp.asarray(a).astype(MXU_DTYPE)
    return c(ovt), c(eful), c(tri), c(ones)


def kernel(x, c, w_ada, b_ada, g_pre_mix, g_post_mix, g_pre_ffn, g_post_ffn, w_in, cmp_pe, cmp_w1, cmp_b1,
           cmp_w2, cmp_b2, g_kv_norm, w_uk, w_uv, w_out, w_gate_up, w_down):
    bsz, s, d = x.shape
    depth = w_ada.shape[0]
    assert s % TM_PROJ == 0 and s % TC == 0 and s // CMP_STRIDE == LANES and s // SLC_BLOCK <= LANES
    assert TQ == WIN_TC and WINDOW % WIN_TC == 0
    cos_t, sa_t, sb_t = _rope_tables(s)
    ovt, eful, tri, ones = _selection_constants(s)
    for l in range(depth):
        lw = _layer_weights(w_in[l], cmp_pe[l], cmp_w1[l], cmp_b1[l], cmp_w2[l], cmp_b2[l], w_uk[l], w_uv[l], s)
        mod3 = _ada_call(c, w_ada[l], b_ada[l]).reshape(bsz, 6, d)
        nq, ncv, nkv, dq, iq, kcat, kidx, misc = _inproj_call(
            x, mod3, g_pre_mix[l][None], g_kv_norm[l][None], lw["w_perm"], cos_t, sa_t, sb_t)
        kcv = _compress_call(ncv, lw["w1a"], lw["w1b"], lw["pea"], lw["peb"], lw["b1"], lw["w2"], lw["b2"])
        o_a = _nsa_call(nq, nkv, kcv, misc, ovt, eful)
        o_b = _dsa_call(dq, iq, misc, kidx, kcat, lw["wq"], lw["wuv"], tri, ones)
        x = _ffn_call(x, o_a, o_b, mod3, g_post_mix[l][None], g_pre_ffn[l][None], g_post_ffn[l][None],
                      w_out[l].astype(MXU_DTYPE), w_gate_up[l].astype(MXU_DTYPE), w_down[l].astype(MXU_DTYPE))
    return x
```

```python
import functools

import numpy as np
import jax
import jax.numpy as jnp
from jax import lax
from jax.experimental import pallas as pl
from jax.experimental.pallas import tpu as pltpu

F32 = jnp.float32
I32 = jnp.int32
MXU_DTYPE = jnp.bfloat16

HEAD_DIM = 64
ROPE_DIM = HEAD_DIM // 4
ROPE_HALF = ROPE_DIM // 2
ROPE_THETA = 500000.0
EPS = 1e-6
NEG_INF = -1e30
LOG2E = 1.4426950408889634
FORCED_SCORE = 1e6
NSA_HEADS = 8
NSA_KV_HEADS = 2
NSA_GROUP = NSA_HEADS // NSA_KV_HEADS
CMP_LEN = 32
CMP_STRIDE = 16
CMP_HIDDEN = 128
SLC_BLOCK = 64
SLC_TOPN = 8
WINDOW = 512
DSA_HEADS = 8
KV_LORA = 128
NOPE_DIM = HEAD_DIM - ROPE_DIM
IDX_HEADS = 8
IDX_DIM = 64
IDX_TOPK_MAX = 256

LANES = 128
TQ = 256
TC = 512
WIN_TC = 256
COUNT_ROWS = 32
HEADS_PER_DOT = 2
TM_PROJ = 512
TM_FFN = 512
FF_CHUNK = 256
INT_MIN = -(2 ** 31)
VMEM_LIMIT = 56 * 1024 * 1024

MISC_IK = 0
MISC_KR = 64
MISC_GATE = 80
MISC_IW = 104
ONES_LANE = 2 * LANES - 1


def _dot(a, b):
    return jnp.dot(a, b, preferred_element_type=F32)


def _dot_nt(a, b):
    return lax.dot_general(a, b, (((1,), (1,)), ((), ())), preferred_element_type=F32)


def _sigmoid(v):
    return 1.0 / (1.0 + jnp.exp(-v))


def _roll_lanes(v, shift):
    return pltpu.roll(v, shift, 1)


def _rep(v, n):
    return v if n == 1 else jnp.concatenate([v] * n, axis=1)


def _ada_kernel(c_ref, w_ref, b_ref, o_ref):
    c = c_ref[...]
    act = (c * _sigmoid(c)).astype(MXU_DTYPE)
    o_ref[...] = _dot(act, w_ref[...].astype(MXU_DTYPE)) + b_ref[...]


def _ada_call(c, w_ada, b_ada):
    bsz, d = c.shape
    n = w_ada.shape[1]
    tn = d
    return pl.pallas_call(
        _ada_kernel,
        grid=(n // tn,),
        in_specs=[pl.BlockSpec((bsz, d), lambda j: (0, 0)),
                  pl.BlockSpec((d, tn), lambda j: (0, j)),
                  pl.BlockSpec((1, tn), lambda j: (0, j))],
        out_specs=pl.BlockSpec((bsz, tn), lambda j: (0, j)),
        out_shape=jax.ShapeDtypeStruct((bsz, n), F32),
        name="adaln_mod",
    )(c, w_ada, b_ada.reshape(1, n))


G_NQ = (0, 512)
G_NCV = (512, 768)
G_NKV = (768, 1280)
G_DQ = (1280, 1792)
G_IQ = (1792, 2304)
G_CKV = (2304, 2432)
G_MISC = (2432, 2560)
IN_PERM_WIDTH = 2560


def _rope_tile(v, cos, sin_a, sin_b):
    return v * cos + _roll_lanes(v, LANES - ROPE_HALF) * sin_a + _roll_lanes(v, ROPE_HALF) * sin_b


def _inproj_kernel(x_ref, mod_ref, g_ref, gkv_ref, w_ref, cos_ref, sa_ref, sb_ref,
                   nq_ref, ncv_ref, nkv_ref, dq_ref, iq_ref, kcat_ref, kidx_ref, misc_ref):
    x = x_ref[0]
    y = x * lax.rsqrt(jnp.mean(x * x, axis=-1, keepdims=True) + EPS) * g_ref[...]
    shift = mod_ref[0, 0:1, :]
    scale = mod_ref[0, 1:2, :]
    h = (y * (1.0 + scale) + shift).astype(MXU_DTYPE)
    cos, sin_a, sin_b = cos_ref[...], sa_ref[...], sb_ref[...]
    qscale = HEAD_DIM ** -0.5 * LOG2E

    def proj(lo, hi):
        return _dot(h, w_ref[:, lo:hi])

    def rope_cols(v, roped):
        tiles = []
        for j in range(v.shape[1] // LANES):
            t = v[:, j * LANES:(j + 1) * LANES]
            tiles.append(_rope_tile(t, cos, sin_a, sin_b) if roped[j] else t)
        return tiles

    for j, t in enumerate(rope_cols(proj(*G_NQ), [True] * 4)):
        nq_ref[0, :, j * LANES:(j + 1) * LANES] = (t * qscale).astype(nq_ref.dtype)
    for j, t in enumerate(rope_cols(proj(*G_NCV), [True, False])):
        ncv_ref[0, :, j * LANES:(j + 1) * LANES] = t
    ks, vs, kw, vw = rope_cols(proj(*G_NKV), [True, False, True, False])
    ones_tile = jnp.ones(ks.shape, F32)
    tok = pl.program_id(1) * x.shape[0] + lax.broadcasted_iota(I32, ks.shape, 0)
    marker = jnp.where(lax.broadcasted_iota(I32, ks.shape, 1) == tok // SLC_BLOCK, NEG_INF, 0.0)
    for j, t in enumerate((ks, marker, vs, ones_tile, kw, vw, ones_tile)):
        nkv_ref[0, :, j * LANES:(j + 1) * LANES] = t.astype(nkv_ref.dtype)
    for j, t in enumerate(rope_cols(proj(*G_DQ), [True] * 4)):
        dq_ref[0, :, j * LANES:(j + 1) * LANES] = (t * qscale).astype(dq_ref.dtype)
    for j, t in enumerate(rope_cols(proj(*G_IQ), [True] * 4)):
        iq_ref[0, :, j * LANES:(j + 1) * LANES] = t.astype(iq_ref.dtype)

    ckv = proj(*G_CKV)
    ckv = ckv * lax.rsqrt(jnp.mean(ckv * ckv, axis=-1, keepdims=True) + EPS) * gkv_ref[...]
    misc = _rope_tile(proj(*G_MISC), cos, sin_a, sin_b)
    misc_ref[0] = misc
    lane = lax.broadcasted_iota(I32, misc.shape, 1)
    kcat_ref[0, :, 0:LANES] = ckv.astype(kcat_ref.dtype)
    kr = jnp.where(lane < ROPE_DIM, _roll_lanes(misc, LANES - MISC_KR),
                   jnp.where(lane == ONES_LANE - LANES, 1.0, 0.0))
    kcat_ref[0, :, LANES:2 * LANES] = kr.astype(kcat_ref.dtype)
    ik = jnp.where(lane < IDX_DIM, misc, 0.0)
    kidx_ref[0, :, 0:LANES] = ik.astype(kidx_ref.dtype)
    kidx_ref[0, :, LANES:2 * LANES] = _roll_lanes(ik, LANES // 2).astype(kidx_ref.dtype)


def _inproj_call(x, mod3, g_pre, g_kv, w_perm, cos_t, sa_t, sb_t):
    bsz, s, d = x.shape
    tm = min(TM_PROJ, s)
    row = lambda width: pl.BlockSpec((1, tm, width), lambda b, i: (b, i, 0))
    const2 = lambda shape: pl.BlockSpec(shape, lambda b, i: (0, 0))
    tab = pl.BlockSpec((tm, LANES), lambda b, i: (i, 0))
    widths = [(512, MXU_DTYPE), (256, F32), (896, MXU_DTYPE), (512, MXU_DTYPE), (512, MXU_DTYPE),
              (256, MXU_DTYPE), (256, MXU_DTYPE), (LANES, F32)]
    return pl.pallas_call(
        _inproj_kernel,
        grid=(bsz, s // tm),
        in_specs=[row(d),
                  pl.BlockSpec((1, 6, d), lambda b, i: (b, 0, 0)),
                  const2((1, d)), const2((1, KV_LORA)), const2((d, IN_PERM_WIDTH)),
                  tab, tab, tab],
        out_specs=[row(w) for w, _ in widths],
        out_shape=[jax.ShapeDtypeStruct((bsz, s, w), dt) for w, dt in widths],
        compiler_params=pltpu.CompilerParams(vmem_limit_bytes=VMEM_LIMIT),
        name="inproj_rope",
    )(x, mod3, g_pre, g_kv, w_perm, cos_t, sa_t, sb_t)


def _gelu_tanh(v):
    return 0.5 * v * (1.0 + jnp.tanh(np.sqrt(2.0 / np.pi) * (v + 0.044715 * (v * v * v))))


def _compress_kernel(xk_ref, xv_ref, w1a_ref, w1b_ref, pea_ref, peb_ref, b1_ref, w2_ref, b2_ref, o_ref):
    n_chunks = xk_ref.shape[1] // CMP_STRIDE
    for kv, x_ref in enumerate((xk_ref, xv_ref)):
        first = jnp.zeros((n_chunks, 2 * CMP_HIDDEN), F32)
        second = jnp.zeros((n_chunks, 2 * CMP_HIDDEN), F32)
        for l in range(CMP_STRIDE):
            slab = x_ref[0, pl.ds(l, n_chunks, stride=CMP_STRIDE), :]
            first = first + _dot((slab + pea_ref[kv, l]).astype(MXU_DTYPE), w1a_ref[kv, l])
            second = second + _dot((slab + peb_ref[kv, l]).astype(MXU_DTYPE), w1b_ref[kv, l])
        hid = first + pltpu.roll(second, n_chunks - 1, 0) + b1_ref[kv]
        out = _dot(_gelu_tanh(hid).astype(MXU_DTYPE), w2_ref[kv]) + b2_ref[kv]
        o_ref[0, :, kv * LANES:(kv + 1) * LANES] = out.astype(o_ref.dtype)


def _compress_call(ncv, w1a, w1b, pea, peb, b1, w2, b2):
    bsz, s, _ = ncv.shape
    n_chunks = s // CMP_STRIDE
    full = lambda a: pl.BlockSpec(a.shape, lambda b: (0,) * a.ndim)
    return pl.pallas_call(
        _compress_kernel,
        grid=(bsz,),
        in_specs=[pl.BlockSpec((1, s, LANES), lambda b: (b, 0, 0)),
                  pl.BlockSpec((1, s, LANES), lambda b: (b, 0, 1)),
                  full(w1a), full(w1b), full(pea), full(peb), full(b1), full(w2), full(b2)],
        out_specs=pl.BlockSpec((1, n_chunks, 2 * LANES), lambda b: (b, 0, 0)),
        out_shape=jax.ShapeDtypeStruct((bsz, n_chunks, 2 * LANES), MXU_DTYPE),
        compiler_params=pltpu.CompilerParams(vmem_limit_bytes=VMEM_LIMIT),
        name="nsa_compress",
    )(ncv, ncv, w1a, w1b, pea, peb, b1, w2, b2)


def _split3(v):
    hi = v.astype(MXU_DTYPE)
    r1 = v - hi.astype(F32)
    mid = r1.astype(MXU_DTYPE)
    lo = (r1 - mid.astype(F32)).astype(MXU_DTYPE)
    return hi, mid, lo


def _attn_init(m_ref, acc_ref):
    m_ref[...] = jnp.full(m_ref.shape, NEG_INF, F32)
    acc_ref[...] = jnp.zeros(acc_ref.shape, F32)


def _attn_chunk(q_heads, k, v, bias, m_ref, acc_ref):
    nh = m_ref.shape[0]
    tc = k.shape[0]
    n_lt = tc // LANES

    def scores(h0):
        return _dot_nt(q_heads(h0, h0 + HEADS_PER_DOT), k)

    s_next = scores(0)
    for h0 in range(0, nh, HEADS_PER_DOT):
        s = s_next
        if h0 + HEADS_PER_DOT < nh:
            s_next = scores(h0 + HEADS_PER_DOT)
        es, alphas = [], []
        for i in range(HEADS_PER_DOT):
            h = h0 + i
            sh = s[i * TQ:(i + 1) * TQ, :]
            if bias is not None:
                sh = sh + bias
            m_old = m_ref[h]
            m_new = jnp.maximum(m_old, jnp.max(sh, axis=-1, keepdims=True))
            alpha = jnp.exp2(m_old - m_new)
            m_ref[h] = m_new
            tiles = [jnp.exp2(sh[:, j * LANES:(j + 1) * LANES] - m_new) for j in range(n_lt)]
            es.append(jnp.concatenate(tiles, axis=1).astype(MXU_DTYPE))
            alphas.append(alpha)
        pv = _dot(jnp.concatenate(es, axis=0), v)
        for i in range(HEADS_PER_DOT):
            for j in range(acc_ref.shape[2] // LANES):
                sl = slice(j * LANES, (j + 1) * LANES)
                acc_ref[h0 + i, :, sl] = alphas[i] * acc_ref[h0 + i, :, sl] + pv[i * TQ:(i + 1) * TQ, sl]


def _merge_groups(x, lane):
    half = LANES // 2
    return [jnp.where(lane < half, x[p, :, 0:LANES], x[p + NSA_GROUP, :, 0:LANES]) for p in range(NSA_GROUP)]


def _nsa_kernel(q_ref, kv_ref, kcv_ref, misc_ref, ovt_ref, gexp_ref, o_ref,
                qs_ref, oc_ref, os_ref, m_ref, acc_ref, *, n_slc):
    qb = pl.program_id(1)
    t0 = qb * TQ
    n_ch = (t0 + TQ + TC - 1) // TC
    half = LANES // 2
    row = lax.broadcasted_iota(I32, (TQ, LANES), 0)
    lane = lax.broadcasted_iota(I32, (TQ, LANES), 1)
    t_abs = t0 + row

    for hd in range(NSA_HEADS):
        g, p = hd // NSA_GROUP, hd % NSA_GROUP
        t = q_ref[0, :, p * LANES:(p + 1) * LANES]
        in_g = (lane >= half * g) & (lane < half * (g + 1))
        qs_ref[hd * TQ:(hd + 1) * TQ, 0:LANES] = jnp.where(in_g, t, jnp.zeros_like(t))
    q_plain = lambda h0, h1: qs_ref[h0 * TQ:h1 * TQ, 0:LANES]
    q_marked = lambda h0, h1: qs_ref[h0 * TQ:h1 * TQ, :]

    kc = kcv_ref[0, :, 0:LANES]
    vc = kcv_ref[0, :, LANES:2 * LANES]
    mask_c = (lane * CMP_STRIDE + (CMP_LEN - 1)) <= t_abs
    s3 = jnp.where(mask_c[None], _dot_nt(q_plain(0, NSA_HEADS), kc).reshape(NSA_HEADS, TQ, LANES), NEG_INF)
    mx = jnp.max(s3, axis=-1, keepdims=True)
    e_c = jnp.where(mask_c[None], jnp.exp2(s3 - mx), 0.0)
    p_c = e_c / jnp.maximum(jnp.sum(e_c, axis=-1, keepdims=True), 1e-30)
    o_c = _dot(p_c.reshape(NSA_HEADS * TQ, LANES).astype(MXU_DTYPE), vc).reshape(NSA_HEADS, TQ, LANES)
    for p, t in enumerate(_merge_groups(o_c, lane)):
        oc_ref[p] = t

    blk_row = lax.broadcasted_iota(I32, (LANES, TQ), 0)
    blk_t = (t0 + lax.broadcasted_iota(I32, (LANES, TQ), 1)) // SLC_BLOCK
    visible = blk_row <= blk_t
    forced = (blk_row == 0) | (blk_row == blk_t) | (blk_row == blk_t - 1)
    for g in range(NSA_KV_HEADS):
        psum = p_c[g * NSA_GROUP]
        for hh in range(1, NSA_GROUP):
            psum = psum + p_c[g * NSA_GROUP + hh]
        ovt = ovt_ref[...]
        imp = sum(_dot_nt(ovt, piece) for piece in _split3(psum))
        imp = jnp.where(visible, jnp.where(forced, FORCED_SCORE, imp), -jnp.inf)
        rank = jnp.zeros((LANES, TQ), I32)
        for jp in range(n_slc):
            other = imp[jp:jp + 1, :]
            beats = (other > imp) | ((other == imp) & (blk_row > jp))
            rank = rank + jnp.where(beats, 1, 0)
        unsel_t = jnp.where((rank >= min(SLC_TOPN, n_slc)) & (blk_row < n_slc), 1.0, 0.0)
        unsel = unsel_t.T.astype(MXU_DTYPE)
        for hh in range(NSA_GROUP):
            hd = g * NSA_GROUP + hh
            qs_ref[hd * TQ:(hd + 1) * TQ, LANES:2 * LANES] = unsel

    def sel_chunk(c, bias):
        start = pl.multiple_of(c * TC, TC)
        k = kv_ref[0, pl.ds(start, TC), 0:2 * LANES]
        v = kv_ref[0, pl.ds(start, TC), 2 * LANES:4 * LANES]
        _attn_chunk(q_marked, k, v, bias, m_ref, acc_ref)

    def sel_body(c, carry):
        sel_chunk(c, None)
        return carry

    def merged_result():
        acc = acc_ref[...]
        num = _merge_groups(acc, lane)
        den = _merge_groups(acc[:, :, LANES:2 * LANES], lane)
        return [n / jnp.maximum(d, 1e-30) for n, d in zip(num, den)]

    _attn_init(m_ref, acc_ref)
    lax.fori_loop(0, n_ch - 1, sel_body, 0)
    last = (n_ch - 1) * TC
    causal = jnp.concatenate([jnp.where(last + j * LANES + lane <= t_abs, 0.0, NEG_INF)
                              for j in range(TC // LANES)], axis=1)
    sel_chunk(n_ch - 1, causal)
    for p, t in enumerate(merged_result()):
        os_ref[p] = t

    def win_body(c, carry):
        start = pl.multiple_of(c * WIN_TC, WIN_TC)
        k = kv_ref[0, pl.ds(start, WIN_TC), 4 * LANES:5 * LANES]
        v = kv_ref[0, pl.ds(start, WIN_TC), 5 * LANES:7 * LANES]
        tiles = []
        for j in range(WIN_TC // LANES):
            kpos = start + j * LANES + lane
            tiles.append(jnp.where((kpos <= t_abs) & (kpos > t_abs - WINDOW), 0.0, NEG_INF))
        _attn_chunk(q_plain, k, v, jnp.concatenate(tiles, axis=1), m_ref, acc_ref)
        return carry

    _attn_init(m_ref, acc_ref)
    lax.fori_loop(jnp.maximum(qb - WINDOW // WIN_TC, 0), qb + 1, win_body, 0)
    o_w = merged_result()

    hi, mid, _ = _split3(_sigmoid(misc_ref[0]))
    gates = _dot(jnp.concatenate([hi, mid], axis=1), gexp_ref[...])
    for p in range(NSA_GROUP):
        gate = lambda br: gates[:, (br * NSA_GROUP + p) * LANES:(br * NSA_GROUP + p + 1) * LANES]
        out = gate(0) * oc_ref[p] + gate(1) * os_ref[p] + gate(2) * o_w[p]
        o_ref[0, :, p * LANES:(p + 1) * LANES] = out.astype(o_ref.dtype)


def _nsa_call(nq, nkv, kcv, misc, ovt, gexp):
    bsz, s, _ = nq.shape
    rowspec = lambda width: pl.BlockSpec((1, TQ, width), lambda b, i: (b, i, 0))
    batch = lambda a: pl.BlockSpec((1,) + a.shape[1:], lambda b, i: (b, 0, 0))
    const = lambda a: pl.BlockSpec(a.shape, lambda b, i: (0, 0))
    merged = pltpu.VMEM((NSA_GROUP, TQ, LANES), F32)
    return pl.pallas_call(
        functools.partial(_nsa_kernel, n_slc=s // SLC_BLOCK),
        grid=(bsz, s // TQ),
        in_specs=[rowspec(512), batch(nkv), batch(kcv), rowspec(LANES), const(ovt), const(gexp)],
        out_specs=rowspec(512),
        out_shape=jax.ShapeDtypeStruct((bsz, s, 512), MXU_DTYPE),
        scratch_shapes=[pltpu.VMEM((NSA_HEADS * TQ, 2 * LANES), MXU_DTYPE),
                        merged, merged,
                        pltpu.VMEM((NSA_HEADS, TQ, LANES), F32),
                        pltpu.VMEM((NSA_HEADS, TQ, 2 * LANES), F32)],
        compiler_params=pltpu.CompilerParams(vmem_limit_bytes=VMEM_LIMIT),
        name="nsa_attention",
    )(nq, nkv, kcv, misc, ovt, gexp)


def _dsa_kernel(dq_ref, iq_ref, misc_ref, kidx_ref, kcat_ref, wq_ref, wuv_ref, tri_ref, ones_ref, o_ref,
                keys_ref, keys_t_ref, bias_ref, qcat_ref, iqs_ref, wbc_ref, m_ref, acc_ref,
                *, k_sel):
    qb = pl.program_id(1)
    t0 = qb * TQ
    n_ch = (t0 + TQ + TC - 1) // TC
    n_lt = TC // LANES
    row = lax.broadcasted_iota(I32, (TQ, LANES), 0)
    lane = lax.broadcasted_iota(I32, (TQ, LANES), 1)
    t_abs = t0 + row

    for pair in range(DSA_HEADS // 2):
        cols = slice(pair * LANES, (pair + 1) * LANES)
        qc = _dot(dq_ref[0, :, cols], wq_ref[pair]).astype(MXU_DTYPE)
        qcat_ref[(2 * pair) * TQ:(2 * pair + 1) * TQ, :] = qc[:, 0:2 * LANES]
        qcat_ref[(2 * pair + 1) * TQ:(2 * pair + 2) * TQ, :] = qc[:, 2 * LANES:4 * LANES]
        iqs_ref[pair * TQ:(pair + 1) * TQ, :] = iq_ref[0, :, cols]
    w_idx = misc_ref[0] * (IDX_HEADS ** -0.5 * IDX_DIM ** -0.5)
    for hd in range(IDX_HEADS):
        wbc_ref[hd] = jnp.broadcast_to(w_idx[:, MISC_IW + hd:MISC_IW + hd + 1], (TQ, LANES))

    def idx_body(c, carry):
        start = pl.multiple_of(c * TC, TC)
        lg_even = _dot_nt(iqs_ref[...], kidx_ref[0, pl.ds(start, TC), 0:LANES])
        lg_odd = _dot_nt(iqs_ref[...], kidx_ref[0, pl.ds(start, TC), LANES:2 * LANES])
        for j in range(n_lt):
            cols = slice(j * LANES, (j + 1) * LANES)
            score = jnp.zeros((TQ, LANES), F32)
            for pair in range(IDX_HEADS // 2):
                rows = slice(pair * TQ, (pair + 1) * TQ)
                score = score + wbc_ref[2 * pair] * jnp.maximum(lg_even[rows, cols], 0.0)
                score = score + wbc_ref[2 * pair + 1] * jnp.maximum(lg_odd[rows, cols], 0.0)
            bits = lax.bitcast_convert_type(score, I32)
            key = bits ^ ((bits >> 31) & 0x7FFFFFFF)
            key = jnp.where(score == 0.0, 0, key)
            keys_ref[c, :, cols] = jnp.where(start + j * LANES + lane <= t_abs, key, INT_MIN)
        keys_t_ref[c] = keys_ref[c].T
        return carry

    lax.fori_loop(0, n_ch, idx_body, 0)

    def count(pred):
        def body(c, part):
            hit = jnp.where(pred(keys_t_ref[c]), 1.0, 0.0)
            return part + jnp.sum(hit.reshape(TC // COUNT_ROWS, COUNT_ROWS, TQ), axis=0)
        part = lax.fori_loop(0, n_ch, body, jnp.zeros((COUNT_ROWS, TQ), F32))
        return jnp.sum(part, axis=0, keepdims=True)

    kf = float(k_sel)
    zero = jnp.zeros((1, TQ), I32)
    thr_t = jnp.where(count(lambda k: k >= zero) >= kf, zero, jnp.full((1, TQ), INT_MIN, I32))

    def bit_body(i, thr_t):
        cand = thr_t | (1 << (30 - i))
        return jnp.where(count(lambda k: k >= cand) >= kf, cand, thr_t)

    thr_t = lax.fori_loop(0, 31, bit_body, thr_t)
    thr_t = jnp.maximum(thr_t, INT_MIN + 1)
    need_t = kf - count(lambda k: k > thr_t)
    thr = jnp.broadcast_to(thr_t, (LANES, TQ)).T
    need = jnp.broadcast_to(need_t, (LANES, TQ)).T
    thr_w = _rep(thr, n_lt)
    need_w = _rep(need, n_lt)

    def bias_body(c, run):
        key = keys_ref[c]
        eq = key == thr_w
        eqf = jnp.where(eq, 1.0, 0.0).astype(MXU_DTYPE)
        prefix = _dot(eqf, tri_ref[...])
        take = (key > thr_w) | (eq & (_rep(run, n_lt) + prefix <= need_w))
        bias_ref[c] = jnp.where(take, 0.0, NEG_INF)
        return run + _dot(eqf, ones_ref[...])

    lax.fori_loop(0, n_ch, bias_body, jnp.zeros((TQ, LANES), F32))

    def att_body(c, carry):
        start = pl.multiple_of(c * TC, TC)
        kc = kcat_ref[0, pl.ds(start, TC), :]
        _attn_chunk(lambda h0, h1: qcat_ref[h0 * TQ:h1 * TQ, :], kc, kc, bias_ref[c], m_ref, acc_ref)
        return carry

    _attn_init(m_ref, acc_ref)
    lax.fori_loop(0, n_ch, att_body, 0)

    for pair in range(DSA_HEADS // 2):
        outs = []
        for hd in (2 * pair, 2 * pair + 1):
            a = acc_ref[hd]
            outs.append(a[:, 0:LANES] / jnp.maximum(a[:, ONES_LANE:ONES_LANE + 1], 1e-30))
        both = jnp.concatenate(outs, axis=1).astype(MXU_DTYPE)
        o_ref[0, :, pair * LANES:(pair + 1) * LANES] = _dot(both, wuv_ref[pair]).astype(o_ref.dtype)


def _dsa_call(dq, iq, misc, kidx, kcat, wq, wuv, tri, ones):
    bsz, s, _ = dq.shape
    n_ch = s // TC
    k_sel = min(IDX_TOPK_MAX, s // 4)
    rowspec = lambda width: pl.BlockSpec((1, TQ, width), lambda b, i: (b, i, 0))
    batch = lambda a: pl.BlockSpec((1,) + a.shape[1:], lambda b, i: (b, 0, 0))
    const = lambda a: pl.BlockSpec(a.shape, lambda b, i: (0,) * a.ndim)
    stat = pltpu.VMEM((DSA_HEADS, TQ, LANES), F32)
    return pl.pallas_call(
        functools.partial(_dsa_kernel, k_sel=k_sel),
        grid=(bsz, s // TQ),
        in_specs=[rowspec(512), rowspec(512), rowspec(LANES), batch(kidx), batch(kcat),
                  const(wq), const(wuv), const(tri), const(ones)],
        out_specs=rowspec(512),
        out_shape=jax.ShapeDtypeStruct((bsz, s, 512), MXU_DTYPE),
        scratch_shapes=[pltpu.VMEM((n_ch, TQ, TC), I32),
                        pltpu.VMEM((n_ch, TC, TQ), I32),
                        pltpu.VMEM((n_ch, TQ, TC), F32),
                        pltpu.VMEM((DSA_HEADS * TQ, 2 * LANES), MXU_DTYPE),
                        pltpu.VMEM((IDX_HEADS // 2 * TQ, LANES), MXU_DTYPE),
                        stat,
                        stat,
                        pltpu.VMEM((DSA_HEADS, TQ, 2 * LANES), F32)],
        compiler_params=pltpu.CompilerParams(vmem_limit_bytes=VMEM_LIMIT),
        name="dsa_attention",
    )(dq, iq, misc, kidx, kcat, wq, wuv, tri, ones)


def _rms(v, g):
    return v * lax.rsqrt(jnp.mean(v * v, axis=-1, keepdims=True) + EPS) * g


def _ffn_kernel(x_ref, oa_ref, ob_ref, mod_ref, gpm_ref, gpf_ref, gqf_ref, wout_ref, wgu_ref, wd_ref,
                o_ref, *, d_ff):
    x = x_ref[0]
    half = oa_ref.shape[2]
    y = _dot(oa_ref[0], wout_ref[0:half, :]) + _dot(ob_ref[0], wout_ref[half:2 * half, :])
    x1 = x + mod_ref[0, 2:3, :] * _rms(y, gpm_ref[...])
    h = (_rms(x1, gpf_ref[...]) * (1.0 + mod_ref[0, 4:5, :]) + mod_ref[0, 3:4, :]).astype(MXU_DTYPE)
    acc = jnp.zeros(x.shape, F32)
    for c in range(d_ff // FF_CHUNK):
        lo = c * FF_CHUNK
        gate = _dot(h, wgu_ref[:, lo:lo + FF_CHUNK])
        up = _dot(h, wgu_ref[:, d_ff + lo:d_ff + lo + FF_CHUNK])
        act = (gate * _sigmoid(gate) * up).astype(MXU_DTYPE)
        acc = acc + _dot(act, wd_ref[lo:lo + FF_CHUNK, :])
    o_ref[0] = x1 + mod_ref[0, 5:6, :] * _rms(acc, gqf_ref[...])


def _ffn_call(x, o_a, o_b, mod3, g_post_mix, g_pre_ffn, g_post_ffn, w_out, w_gu, w_down):
    bsz, s, d = x.shape
    d_ff = w_down.shape[0]
    tm = min(TM_FFN, s)
    row = lambda width: pl.BlockSpec((1, tm, width), lambda b, i: (b, i, 0))
    const = lambda a: pl.BlockSpec(a.shape, lambda b, i: (0, 0), pipeline_mode=pl.Buffered(1))
    return pl.pallas_call(
        functools.partial(_ffn_kernel, d_ff=d_ff),
        grid=(bsz, s // tm),
        in_specs=[row(d), row(o_a.shape[2]), row(o_b.shape[2]),
                  pl.BlockSpec((1, 6, d), lambda b, i: (b, 0, 0)),
                  const(g_post_mix), const(g_pre_ffn), const(g_post_ffn),
                  const(w_out), const(w_gu), const(w_down)],
        out_specs=row(d),
        out_shape=jax.ShapeDtypeStruct((bsz, s, d), x.dtype),
        compiler_params=pltpu.CompilerParams(vmem_limit_bytes=VMEM_LIMIT),
        name="outproj_ffn",
    )(x, o_a, o_b, mod3, g_post_mix, g_pre_ffn, g_post_ffn, w_out, w_gu, w_down)


def _in_perm_indices():
    splits = (NSA_HEADS * HEAD_DIM,) + (NSA_KV_HEADS * HEAD_DIM,) * 6 + (
        3 * NSA_HEADS, DSA_HEADS * HEAD_DIM, KV_LORA, ROPE_DIM, IDX_HEADS * IDX_DIM, IDX_DIM, IDX_HEADS)
    off = np.concatenate([[0], np.cumsum(splits)])
    seg = lambda i: np.arange(off[i], off[i + 1])
    (nq, nkc, nvc, nks, nvs, nkw, nvw, ngate, dq, dckv, dkr, iq, ik, iw) = [seg(i) for i in range(14)]
    nq = nq[_nsa_out_rows()]
    return np.concatenate([nq, nkc, nvc, nks, nvs, nkw, nvw, dq, iq, dckv, ik, dkr, ngate, iw])


def _rope_tables(s):
    inv_freq = 1.0 / (ROPE_THETA ** (jnp.arange(0, ROPE_DIM, 2, dtype=F32) / ROPE_DIM))
    ang = jnp.arange(s, dtype=F32)[:, None] * inv_freq[None, :]
    cos, sin = jnp.cos(ang), jnp.sin(ang)
    ones = jnp.ones((s, HEAD_DIM - ROPE_DIM), F32)
    zeros8 = jnp.zeros((s, ROPE_HALF), F32)
    zeros = jnp.zeros((s, HEAD_DIM - ROPE_DIM), F32)
    cos_h = jnp.concatenate([cos, cos, ones], axis=1)
    sa_h = jnp.concatenate([-sin, zeros8, zeros], axis=1)
    sb_h = jnp.concatenate([zeros8, sin, zeros], axis=1)
    two = lambda t: jnp.concatenate([t, t], axis=1)
    return two(cos_h), two(sa_h), two(sb_h)


def _blockdiag2(a, b):
    za = jnp.zeros((a.shape[0], b.shape[1]), a.dtype)
    zb = jnp.zeros((b.shape[0], a.shape[1]), a.dtype)
    return jnp.concatenate([jnp.concatenate([a, za], axis=1), jnp.concatenate([zb, b], axis=1)], axis=0)


def _layer_weights(w_in, cmp_pe, cmp_w1, cmp_b1, cmp_w2, cmp_b2, w_uk, w_uv, s):
    d = w_in.shape[0]
    perm = _in_perm_indices()
    w_perm = jnp.concatenate([w_in[:, perm], jnp.zeros((d, IN_PERM_WIDTH - perm.size), w_in.dtype)], axis=1)
    w1 = cmp_w1.reshape(2, CMP_LEN, HEAD_DIM, CMP_HIDDEN)
    bd = jax.vmap(jax.vmap(lambda m: _blockdiag2(m, m)))
    w1a, w1b = bd(w1[:, :CMP_STRIDE]), bd(w1[:, CMP_STRIDE:])
    pe2 = jnp.concatenate([cmp_pe, cmp_pe], axis=-1)[:, :, None, :]
    pea, peb = pe2[:, :CMP_STRIDE], pe2[:, CMP_STRIDE:]
    b1 = jnp.concatenate([cmp_b1, cmp_b1], axis=-1)[:, None, :]
    w2 = jax.vmap(lambda m: _blockdiag2(m, m))(cmp_w2)
    b2 = jnp.concatenate([cmp_b2, cmp_b2], axis=-1)[:, None, :]
    def head_q(uk):
        top = jnp.concatenate([jnp.zeros((ROPE_DIM, KV_LORA), F32), jnp.eye(ROPE_DIM, dtype=F32),
                               jnp.zeros((ROPE_DIM, LANES - ROPE_DIM), F32)], axis=1)
        bot = jnp.concatenate([uk.T, jnp.zeros((NOPE_DIM, LANES), F32)], axis=1)
        return jnp.concatenate([top, bot], axis=0)
    hq = jax.vmap(head_q)(w_uk)
    wq = jax.vmap(_blockdiag2)(hq[0::2], hq[1::2])
    wuv = jax.vmap(_blockdiag2)(w_uv[0::2], w_uv[1::2])
    c = lambda a: a.astype(MXU_DTYPE)
    return dict(w_perm=c(w_perm), w1a=c(w1a), w1b=c(w1b), pea=pea, peb=peb, b1=b1, w2=c(w2), b2=b2,
                wq=c(wq), wuv=c(wuv))


def _selection_constants(s):
    n_cmp = (s - CMP_LEN) // CMP_STRIDE + 1
    n_slc = s // SLC_BLOCK
    ci = np.arange(n_cmp)[:, None] * CMP_STRIDE
    sj = np.arange(n_slc)[None, :] * SLC_BLOCK
    ovt = np.zeros((LANES, s // CMP_STRIDE), np.float32)
    ovt[:n_slc, :n_cmp] = ((ci < sj + SLC_BLOCK) & (ci + CMP_LEN > sj)).astype(np.float32).T
    tri = np.triu(np.ones((TC, TC), np.float32))
    ones = np.ones((TC, LANES), np.float32)
    gexp = np.zeros((LANES, 3 * NSA_GROUP * LANES), np.float32)
    for br in range(3):
        for hd in range(NSA_HEADS):
            lo = (br * NSA_GROUP + hd % NSA_GROUP) * LANES + (hd // NSA_GROUP) * (LANES // 2)
            gexp[MISC_GATE + br * NSA_HEADS + hd, lo:lo + LANES // 2] = 1.0
    gexp = np.concatenate([gexp, gexp], axis=0)
    c = lambda a: jnp.asarray(a).astype(MXU_DTYPE)
    return c(ovt), c(tri), c(ones), c(gexp)


def _nsa_out_rows():
    order = [hd for p in range(NSA_GROUP) for hd in (p, p + NSA_GROUP)]
    return np.concatenate([np.arange(hd * HEAD_DIM, (hd + 1) * HEAD_DIM) for hd in order])


def kernel(x, c, w_ada, b_ada, g_pre_mix, g_post_mix, g_pre_ffn, g_post_ffn, w_in, cmp_pe, cmp_w1, cmp_b1,
           cmp_w2, cmp_b2, g_kv_norm, w_uk, w_uv, w_out, w_gate_up, w_down):
    bsz, s, d = x.shape
    depth = w_ada.shape[0]
    assert s % TM_PROJ == 0 and s % TC == 0 and s // CMP_STRIDE == LANES and s // SLC_BLOCK <= LANES
    assert TQ == WIN_TC and WINDOW % WIN_TC == 0
    cos_t, sa_t, sb_t = _rope_tables(s)
    ovt, tri, ones, gexp = _selection_constants(s)
    out_rows = np.concatenate([_nsa_out_rows(), np.arange(NSA_HEADS * HEAD_DIM, d)])
    for l in range(depth):
        lw = _layer_weights(w_in[l], cmp_pe[l], cmp_w1[l], cmp_b1[l], cmp_w2[l], cmp_b2[l], w_uk[l], w_uv[l], s)
        mod3 = _ada_call(c, w_ada[l], b_ada[l]).reshape(bsz, 6, d)
        nq, ncv, nkv, dq, iq, kcat, kidx, misc = _inproj_call(
            x, mod3, g_pre_mix[l][None], g_kv_norm[l][None], lw["w_perm"], cos_t, sa_t, sb_t)
        kcv = _compress_call(ncv, lw["w1a"], lw["w1b"], lw["pea"], lw["peb"], lw["b1"], lw["w2"], lw["b2"])
        o_a = _nsa_call(nq, nkv, kcv, misc, ovt, gexp)
        o_b = _dsa_call(dq, iq, misc, kidx, kcat, lw["wq"], lw["wuv"], tri, ones)
        x = _ffn_call(x, o_a, o_b, mod3, g_post_mix[l][None], g_pre_ffn[l][None], g_post_ffn[l][None],
                      w_out[l][out_rows].astype(MXU_DTYPE), w_gate_up[l].astype(MXU_DTYPE),
                      w_down[l].astype(MXU_DTYPE))
    return x
```

```python
import functools

import numpy as np
import jax
import jax.numpy as jnp
from jax import lax
from jax.experimental import pallas as pl
from jax.experimental.pallas import tpu as pltpu

F32 = jnp.float32
I32 = jnp.int32
MXU_DTYPE = jnp.bfloat16

HEAD_DIM = 64
ROPE_DIM = HEAD_DIM // 4
ROPE_HALF = ROPE_DIM // 2
ROPE_THETA = 500000.0
EPS = 1e-6
NEG_INF = -1e30
LOG2E = 1.4426950408889634
FORCED_SCORE = 1e6
NSA_HEADS = 8
NSA_KV_HEADS = 2
NSA_GROUP = NSA_HEADS // NSA_KV_HEADS
CMP_LEN = 32
CMP_STRIDE = 16
CMP_HIDDEN = 128
SLC_BLOCK = 64
SLC_TOPN = 8
WINDOW = 512
DSA_HEADS = 8
KV_LORA = 128
NOPE_DIM = HEAD_DIM - ROPE_DIM
IDX_HEADS = 8
IDX_DIM = 64
IDX_TOPK_MAX = 256

LANES = 128
TQ = 256
TQ_DSA = 512
TC = 512
COUNT_ROWS = 32
HEADS_PER_DOT = 2
TM_PROJ = 512
TM_FFN = 512
FF_CHUNK = 256
INT_MIN = -(2 ** 31)
I16 = jnp.int16
I16_MIN = -(2 ** 15)
VMEM_LIMIT = 56 * 1024 * 1024

MISC_IK = 0
MISC_KR = 64
MISC_GATE = 80
MISC_IW = 104
ONES_LANE = 2 * LANES - 1


def _dot(a, b):
    return jnp.dot(a, b, preferred_element_type=F32)


def _dot_nt(a, b):
    return lax.dot_general(a, b, (((1,), (1,)), ((), ())), preferred_element_type=F32)


def _sigmoid(v):
    return 1.0 / (1.0 + jnp.exp(-v))


def _roll_lanes(v, shift):
    return pltpu.roll(v, shift, 1)


def _rep(v, n):
    return v if n == 1 else jnp.concatenate([v] * n, axis=1)


def _ada_kernel(c_ref, w_ref, b_ref, o_ref):
    c = c_ref[...]
    act = (c * _sigmoid(c)).astype(MXU_DTYPE)
    o_ref[...] = _dot(act, w_ref[...].astype(MXU_DTYPE)) + b_ref[...]


def _ada_call(c, w_ada, b_ada):
    bsz, d = c.shape
    n = w_ada.shape[1]
    tn = d
    return pl.pallas_call(
        _ada_kernel,
        grid=(n // tn,),
        in_specs=[pl.BlockSpec((bsz, d), lambda j: (0, 0)),
                  pl.BlockSpec((d, tn), lambda j: (0, j)),
                  pl.BlockSpec((1, tn), lambda j: (0, j))],
        out_specs=pl.BlockSpec((bsz, tn), lambda j: (0, j)),
        out_shape=jax.ShapeDtypeStruct((bsz, n), F32),
        name="adaln_mod",
    )(c, w_ada, b_ada.reshape(1, n))


G_NQ = (0, 512)
G_NCV = (512, 768)
G_NKV = (768, 1280)
G_DQ = (1280, 1792)
G_IQ = (1792, 2304)
G_CKV = (2304, 2432)
G_MISC = (2432, 2560)
IN_PERM_WIDTH = 2560


def _rope_tile(v, cos, sin_a, sin_b):
    return v * cos + _roll_lanes(v, LANES - ROPE_HALF) * sin_a + _roll_lanes(v, ROPE_HALF) * sin_b


def _inproj_kernel(x_ref, mod_ref, g_ref, gkv_ref, w_ref, cos_ref, sa_ref, sb_ref,
                   nq_ref, ncv_ref, nkv_ref, dq_ref, iq_ref, kcat_ref, kidx_ref, misc_ref):
    x = x_ref[0]
    y = x * lax.rsqrt(jnp.mean(x * x, axis=-1, keepdims=True) + EPS) * g_ref[...]
    shift = mod_ref[0, 0:1, :]
    scale = mod_ref[0, 1:2, :]
    h = (y * (1.0 + scale) + shift).astype(MXU_DTYPE)
    cos, sin_a, sin_b = cos_ref[...], sa_ref[...], sb_ref[...]
    qscale = HEAD_DIM ** -0.5 * LOG2E

    def proj(lo, hi):
        return _dot(h, w_ref[:, lo:hi])

    def rope_cols(v, roped):
        tiles = []
        for j in range(v.shape[1] // LANES):
            t = v[:, j * LANES:(j + 1) * LANES]
            tiles.append(_rope_tile(t, cos, sin_a, sin_b) if roped[j] else t)
        return tiles

    for j, t in enumerate(rope_cols(proj(*G_NQ), [True] * 4)):
        nq_ref[0, :, j * LANES:(j + 1) * LANES] = (t * qscale).astype(nq_ref.dtype)
    for j, t in enumerate(rope_cols(proj(*G_NCV), [True, False])):
        ncv_ref[0, :, j * LANES:(j + 1) * LANES] = t
    ks, vs, kw, vw = rope_cols(proj(*G_NKV), [True, False, True, False])
    ones_tile = jnp.ones(ks.shape, F32)
    tok = pl.program_id(1) * x.shape[0] + lax.broadcasted_iota(I32, ks.shape, 0)
    marker = jnp.where(lax.broadcasted_iota(I32, ks.shape, 1) == tok // SLC_BLOCK, NEG_INF, 0.0)
    for j, t in enumerate((ks, marker, vs, ones_tile, kw, vw, ones_tile)):
        nkv_ref[0, :, j * LANES:(j + 1) * LANES] = t.astype(nkv_ref.dtype)
    for j, t in enumerate(rope_cols(proj(*G_DQ), [True] * 4)):
        dq_ref[0, :, j * LANES:(j + 1) * LANES] = (t * qscale).astype(dq_ref.dtype)
    for j, t in enumerate(rope_cols(proj(*G_IQ), [True] * 4)):
        iq_ref[0, :, j * LANES:(j + 1) * LANES] = t.astype(iq_ref.dtype)

    ckv = proj(*G_CKV)
    ckv = ckv * lax.rsqrt(jnp.mean(ckv * ckv, axis=-1, keepdims=True) + EPS) * gkv_ref[...]
    misc = _rope_tile(proj(*G_MISC), cos, sin_a, sin_b)
    misc_ref[0] = misc
    lane = lax.broadcasted_iota(I32, misc.shape, 1)
    kcat_ref[0, :, 0:LANES] = ckv.astype(kcat_ref.dtype)
    kr = jnp.where(lane < ROPE_DIM, _roll_lanes(misc, LANES - MISC_KR),
                   jnp.where(lane == ONES_LANE - LANES, 1.0, 0.0))
    kcat_ref[0, :, LANES:2 * LANES] = kr.astype(kcat_ref.dtype)
    ik = jnp.where(lane < IDX_DIM, misc, 0.0)
    kidx_ref[0, :, 0:LANES] = ik.astype(kidx_ref.dtype)
    kidx_ref[0, :, LANES:2 * LANES] = _roll_lanes(ik, LANES // 2).astype(kidx_ref.dtype)


def _inproj_call(x, mod3, g_pre, g_kv, w_perm, cos_t, sa_t, sb_t):
    bsz, s, d = x.shape
    tm = min(TM_PROJ, s)
    row = lambda width: pl.BlockSpec((1, tm, width), lambda b, i: (b, i, 0))
    const2 = lambda shape: pl.BlockSpec(shape, lambda b, i: (0, 0))
    tab = pl.BlockSpec((tm, LANES), lambda b, i: (i, 0))
    widths = [(512, MXU_DTYPE), (256, F32), (896, MXU_DTYPE), (512, MXU_DTYPE), (512, MXU_DTYPE),
              (256, MXU_DTYPE), (256, MXU_DTYPE), (LANES, F32)]
    return pl.pallas_call(
        _inproj_kernel,
        grid=(bsz, s // tm),
        in_specs=[row(d),
                  pl.BlockSpec((1, 6, d), lambda b, i: (b, 0, 0)),
                  const2((1, d)), const2((1, KV_LORA)), const2((d, IN_PERM_WIDTH)),
                  tab, tab, tab],
        out_specs=[row(w) for w, _ in widths],
        out_shape=[jax.ShapeDtypeStruct((bsz, s, w), dt) for w, dt in widths],
        compiler_params=pltpu.CompilerParams(vmem_limit_bytes=VMEM_LIMIT),
        name="inproj_rope",
    )(x, mod3, g_pre, g_kv, w_perm, cos_t, sa_t, sb_t)


def _gelu_tanh(v):
    return 0.5 * v * (1.0 + jnp.tanh(np.sqrt(2.0 / np.pi) * (v + 0.044715 * (v * v * v))))


def _compress_kernel(xk_ref, xv_ref, w1a_ref, w1b_ref, pea_ref, peb_ref, b1_ref, w2_ref, b2_ref, o_ref):
    n_chunks = xk_ref.shape[1] // CMP_STRIDE
    for kv, x_ref in enumerate((xk_ref, xv_ref)):
        first = jnp.zeros((n_chunks, 2 * CMP_HIDDEN), F32)
        second = jnp.zeros((n_chunks, 2 * CMP_HIDDEN), F32)
        for l in range(CMP_STRIDE):
            slab = x_ref[0, pl.ds(l, n_chunks, stride=CMP_STRIDE), :]
            first = first + _dot((slab + pea_ref[kv, l]).astype(MXU_DTYPE), w1a_ref[kv, l])
            second = second + _dot((slab + peb_ref[kv, l]).astype(MXU_DTYPE), w1b_ref[kv, l])
        hid = first + pltpu.roll(second, n_chunks - 1, 0) + b1_ref[kv]
        out = _dot(_gelu_tanh(hid).astype(MXU_DTYPE), w2_ref[kv]) + b2_ref[kv]
        o_ref[0, :, kv * LANES:(kv + 1) * LANES] = out.astype(o_ref.dtype)


def _compress_call(ncv, w1a, w1b, pea, peb, b1, w2, b2):
    bsz, s, _ = ncv.shape
    n_chunks = s // CMP_STRIDE
    full = lambda a: pl.BlockSpec(a.shape, lambda b: (0,) * a.ndim)
    return pl.pallas_call(
        _compress_kernel,
        grid=(bsz,),
        in_specs=[pl.BlockSpec((1, s, LANES), lambda b: (b, 0, 0)),
                  pl.BlockSpec((1, s, LANES), lambda b: (b, 0, 1)),
                  full(w1a), full(w1b), full(pea), full(peb), full(b1), full(w2), full(b2)],
        out_specs=pl.BlockSpec((1, n_chunks, 2 * LANES), lambda b: (b, 0, 0)),
        out_shape=jax.ShapeDtypeStruct((bsz, n_chunks, 2 * LANES), MXU_DTYPE),
        compiler_params=pltpu.CompilerParams(vmem_limit_bytes=VMEM_LIMIT),
        name="nsa_compress",
    )(ncv, ncv, w1a, w1b, pea, peb, b1, w2, b2)


def _split3(v):
    hi = v.astype(MXU_DTYPE)
    r1 = v - hi.astype(F32)
    mid = r1.astype(MXU_DTYPE)
    lo = (r1 - mid.astype(F32)).astype(MXU_DTYPE)
    return hi, mid, lo


def _attn_init(m_ref, acc_ref):
    m_ref[...] = jnp.full(m_ref.shape, NEG_INF, F32)
    acc_ref[...] = jnp.zeros(acc_ref.shape, F32)


def _attn_chunk(q_heads, k, v, bias, m_ref, acc_ref):
    nh, tq = m_ref.shape[0], m_ref.shape[1]
    tc = k.shape[0]
    n_lt = tc // LANES

    def scores(h0):
        return _dot_nt(q_heads(h0, h0 + HEADS_PER_DOT), k)

    s_next = scores(0)
    for h0 in range(0, nh, HEADS_PER_DOT):
        s = s_next
        if h0 + HEADS_PER_DOT < nh:
            s_next = scores(h0 + HEADS_PER_DOT)
        es, alphas = [], []
        for i in range(HEADS_PER_DOT):
            h = h0 + i
            sh = s[i * tq:(i + 1) * tq, :]
            if bias is not None:
                sh = sh + bias
            m_old = m_ref[h]
            m_new = jnp.maximum(m_old, jnp.max(sh, axis=-1, keepdims=True))
            alpha = jnp.exp2(m_old - m_new)
            m_ref[h] = m_new
            tiles = [jnp.exp2(sh[:, j * LANES:(j + 1) * LANES] - m_new) for j in range(n_lt)]
            es.append(jnp.concatenate(tiles, axis=1).astype(MXU_DTYPE))
            alphas.append(alpha)
        pv = _dot(jnp.concatenate(es, axis=0), v)
        for i in range(HEADS_PER_DOT):
            for j in range(acc_ref.shape[2] // LANES):
                sl = slice(j * LANES, (j + 1) * LANES)
                acc_ref[h0 + i, :, sl] = alphas[i] * acc_ref[h0 + i, :, sl] + pv[i * tq:(i + 1) * tq, sl]


def _merge_groups(x, lane):
    half = LANES // 2
    return [jnp.where(lane < half, x[p, :, 0:LANES], x[p + NSA_GROUP, :, 0:LANES]) for p in range(NSA_GROUP)]


def _nsa_kernel(q_ref, kv_ref, kcv_ref, misc_ref, ovt_ref, gexp_ref, o_ref,
                qs_ref, oc_ref, os_ref, m_ref, acc_ref, *, n_slc):
    qb = pl.program_id(1)
    t0 = qb * TQ
    n_ch = (t0 + TQ + TC - 1) // TC
    half = LANES // 2
    row = lax.broadcasted_iota(I32, (TQ, LANES), 0)
    lane = lax.broadcasted_iota(I32, (TQ, LANES), 1)
    t_abs = t0 + row

    for hd in range(NSA_HEADS):
        g, p = hd // NSA_GROUP, hd % NSA_GROUP
        t = q_ref[0, :, p * LANES:(p + 1) * LANES]
        in_g = (lane >= half * g) & (lane < half * (g + 1))
        qs_ref[hd * TQ:(hd + 1) * TQ, 0:LANES] = jnp.where(in_g, t, jnp.zeros_like(t))
    q_plain = lambda h0, h1: qs_ref[h0 * TQ:h1 * TQ, 0:LANES]
    q_marked = lambda h0, h1: qs_ref[h0 * TQ:h1 * TQ, :]

    kc = kcv_ref[0, :, 0:LANES]
    vc = kcv_ref[0, :, LANES:2 * LANES]
    mask_c = (lane * CMP_STRIDE + (CMP_LEN - 1)) <= t_abs
    s3 = jnp.where(mask_c[None], _dot_nt(q_plain(0, NSA_HEADS), kc).reshape(NSA_HEADS, TQ, LANES), NEG_INF)
    mx = jnp.max(s3, axis=-1, keepdims=True)
    e_c = jnp.where(mask_c[None], jnp.exp2(s3 - mx), 0.0)
    p_c = e_c / jnp.maximum(jnp.sum(e_c, axis=-1, keepdims=True), 1e-30)
    o_c = _dot(p_c.reshape(NSA_HEADS * TQ, LANES).astype(MXU_DTYPE), vc).reshape(NSA_HEADS, TQ, LANES)
    for p, t in enumerate(_merge_groups(o_c, lane)):
        oc_ref[p] = t

    blk_row = lax.broadcasted_iota(I32, (LANES, TQ), 0)
    blk_t = (t0 + lax.broadcasted_iota(I32, (LANES, TQ), 1)) // SLC_BLOCK
    visible = blk_row <= blk_t
    forced = (blk_row == 0) | (blk_row == blk_t) | (blk_row == blk_t - 1)
    for g in range(NSA_KV_HEADS):
        psum = p_c[g * NSA_GROUP]
        for hh in range(1, NSA_GROUP):
            psum = psum + p_c[g * NSA_GROUP + hh]
        ovt = ovt_ref[...]
        imp = sum(_dot_nt(ovt, piece) for piece in _split3(psum))
        imp = jnp.where(visible, jnp.where(forced, FORCED_SCORE, imp), -jnp.inf)
        rank = jnp.zeros((LANES, TQ), I32)
        for jp in range(n_slc):
            other = imp[jp:jp + 1, :]
            beats = (other > imp) | ((other == imp) & (blk_row > jp))
            rank = rank + jnp.where(beats, 1, 0)
        unsel_t = jnp.where((rank >= min(SLC_TOPN, n_slc)) & (blk_row < n_slc), 1.0, 0.0)
        unsel = unsel_t.T.astype(MXU_DTYPE)
        for hh in range(NSA_GROUP):
            hd = g * NSA_GROUP + hh
            qs_ref[hd * TQ:(hd + 1) * TQ, LANES:2 * LANES] = unsel

    def sel_chunk(c, bias):
        start = pl.multiple_of(c * TC, TC)
        k = kv_ref[0, pl.ds(start, TC), 0:2 * LANES]
        v = kv_ref[0, pl.ds(start, TC), 2 * LANES:4 * LANES]
        _attn_chunk(q_marked, k, v, bias, m_ref, acc_ref)

    def sel_body(c, carry):
        sel_chunk(c, None)
        return carry

    def merged_result():
        acc = acc_ref[...]
        num = _merge_groups(acc, lane)
        den = _merge_groups(acc[:, :, LANES:2 * LANES], lane)
        return [n / jnp.maximum(d, 1e-30) for n, d in zip(num, den)]

    _attn_init(m_ref, acc_ref)
    lax.fori_loop(0, n_ch - 1, sel_body, 0)
    last = (n_ch - 1) * TC
    causal = jnp.concatenate([jnp.where(last + j * LANES + lane <= t_abs, 0.0, NEG_INF)
                              for j in range(TC // LANES)], axis=1)
    sel_chunk(n_ch - 1, causal)
    for p, t in enumerate(merged_result()):
        os_ref[p] = t

    wstart = pl.multiple_of(jnp.maximum(t0 - WINDOW, 0), TQ)
    kw = kv_ref[0, pl.ds(wstart, WINDOW + TQ), 4 * LANES:5 * LANES]
    vw = kv_ref[0, pl.ds(wstart, WINDOW + TQ), 5 * LANES:7 * LANES]
    tiles = []
    for j in range((WINDOW + TQ) // LANES):
        kpos = wstart + j * LANES + lane
        tiles.append(jnp.where((kpos <= t_abs) & (kpos > t_abs - WINDOW), 0.0, NEG_INF))
    _attn_init(m_ref, acc_ref)
    _attn_chunk(q_plain, kw, vw, jnp.concatenate(tiles, axis=1), m_ref, acc_ref)
    o_w = merged_result()

    hi, mid, _ = _split3(_sigmoid(misc_ref[0]))
    gates = _dot(jnp.concatenate([hi, mid], axis=1), gexp_ref[...])
    for p in range(NSA_GROUP):
        gate = lambda br: gates[:, (br * NSA_GROUP + p) * LANES:(br * NSA_GROUP + p + 1) * LANES]
        out = gate(0) * oc_ref[p] + gate(1) * os_ref[p] + gate(2) * o_w[p]
        o_ref[0, :, p * LANES:(p + 1) * LANES] = out.astype(o_ref.dtype)


def _nsa_call(nq, nkv, kcv, misc, ovt, gexp):
    bsz, s, _ = nq.shape
    rowspec = lambda width: pl.BlockSpec((1, TQ, width), lambda b, i: (b, i, 0))
    batch = lambda a: pl.BlockSpec((1,) + a.shape[1:], lambda b, i: (b, 0, 0))
    const = lambda a: pl.BlockSpec(a.shape, lambda b, i: (0, 0))
    merged = pltpu.VMEM((NSA_GROUP, TQ, LANES), F32)
    return pl.pallas_call(
        functools.partial(_nsa_kernel, n_slc=s // SLC_BLOCK),
        grid=(bsz, s // TQ),
        in_specs=[rowspec(512), batch(nkv), batch(kcv), rowspec(LANES), const(ovt), const(gexp)],
        out_specs=rowspec(512),
        out_shape=jax.ShapeDtypeStruct((bsz, s, 512), MXU_DTYPE),
        scratch_shapes=[pltpu.VMEM((NSA_HEADS * TQ, 2 * LANES), MXU_DTYPE),
                        merged, merged,
                        pltpu.VMEM((NSA_HEADS, TQ, LANES), F32),
                        pltpu.VMEM((NSA_HEADS, TQ, 2 * LANES), F32)],
        compiler_params=pltpu.CompilerParams(vmem_limit_bytes=VMEM_LIMIT),
        name="nsa_attention",
    )(nq, nkv, kcv, misc, ovt, gexp)


def _dsa_kernel(dq_ref, iq_ref, misc_ref, kidx_ref, kcat_ref, wq_ref, wuv_ref, tri_ref, ones_ref, wexp_ref, o_ref,
                keys_ref, hi_t_ref, lo_t_ref, bias_ref, qcat_ref, iqs_ref, wbc_ref, m_ref, acc_ref,
                *, k_sel):
    TQ = TQ_DSA
    qb = pl.program_id(1)
    t0 = qb * TQ
    n_ch = (t0 + TQ + TC - 1) // TC
    n_lt = TC // LANES
    row = lax.broadcasted_iota(I32, (TQ, LANES), 0)
    lane = lax.broadcasted_iota(I32, (TQ, LANES), 1)
    t_abs = t0 + row

    for pair in range(DSA_HEADS // 2):
        cols = slice(pair * LANES, (pair + 1) * LANES)
        qc = _dot(dq_ref[0, :, cols], wq_ref[pair]).astype(MXU_DTYPE)
        qcat_ref[(2 * pair) * TQ:(2 * pair + 1) * TQ, :] = qc[:, 0:2 * LANES]
        qcat_ref[(2 * pair + 1) * TQ:(2 * pair + 2) * TQ, :] = qc[:, 2 * LANES:4 * LANES]
        iqs_ref[pair * TQ:(pair + 1) * TQ, :] = iq_ref[0, :, cols]
    hi, mid, _ = _split3(misc_ref[0] * (IDX_HEADS ** -0.5 * IDX_DIM ** -0.5))
    w_rep = _dot(jnp.concatenate([hi, mid], axis=1), wexp_ref[...])
    for hd in range(IDX_HEADS):
        wbc_ref[hd] = w_rep[:, hd * LANES:(hd + 1) * LANES]

    def idx_body(c, carry):
        start = pl.multiple_of(c * TC, TC)
        half_w = TC // 2

        def logits(part):
            rows = pl.ds(pl.multiple_of(start + part * half_w, half_w), half_w)
            return (_dot_nt(iqs_ref[...], kidx_ref[0, rows, 0:LANES]),
                    _dot_nt(iqs_ref[...], kidx_ref[0, rows, LANES:2 * LANES]))

        lg_next = logits(0)
        for j in range(n_lt):
            part, sub = divmod(j, half_w // LANES)
            if sub == 0:
                lg_even, lg_odd = lg_next
                if part == 0:
                    lg_next = logits(1)
            cols = slice(sub * LANES, (sub + 1) * LANES)
            score = jnp.zeros((TQ, LANES), F32)
            for pair in range(IDX_HEADS // 2):
                rows = slice(pair * TQ, (pair + 1) * TQ)
                score = score + wbc_ref[2 * pair] * jnp.maximum(lg_even[rows, cols], 0.0)
                score = score + wbc_ref[2 * pair + 1] * jnp.maximum(lg_odd[rows, cols], 0.0)
            bits = lax.bitcast_convert_type(score, I32)
            key = bits ^ ((bits >> 31) & 0x7FFFFFFF)
            key = jnp.where(score == 0.0, 0, key)
            keys_ref[c, :, j * LANES:(j + 1) * LANES] = jnp.where(start + j * LANES + lane <= t_abs, key, INT_MIN)
        key_t = keys_ref[c].T
        hi_t_ref[c] = (key_t >> 16).astype(I16)
        lo_t_ref[c] = ((key_t & 0xFFFF) + I16_MIN).astype(I16)
        return carry

    lax.fori_loop(0, n_ch, idx_body, 0)

    def count16(ref, pred):
        def body(c, part):
            hit = jnp.where(pred(ref[c]), jnp.ones((TC, TQ), I16), jnp.zeros((TC, TQ), I16))
            for r in range(TC // COUNT_ROWS):
                part = part + hit[r * COUNT_ROWS:(r + 1) * COUNT_ROWS, :]
            return part
        part = lax.fori_loop(0, n_ch, body, jnp.zeros((COUNT_ROWS, TQ), I16))
        return jnp.sum(part.astype(F32), axis=0, keepdims=True)

    def search16(ref, wanted):
        def ge(cand):
            cand16 = cand.astype(I16)
            return count16(ref, lambda a: a >= cand16)
        zero = jnp.zeros((1, TQ), I32)
        val = jnp.where(ge(zero) >= wanted, zero, jnp.full((1, TQ), I16_MIN, I32))

        def bit_body(i, val):
            cand = val | (1 << (14 - i))
            return jnp.where(ge(cand) >= wanted, cand, val)

        return lax.fori_loop(0, 15, bit_body, val)

    kf = float(k_sel)
    hi_sel = search16(hi_t_ref, kf)
    hi_sel16 = hi_sel.astype(I16)
    above = count16(hi_t_ref, lambda a: a > hi_sel16)

    def lo_body(c, carry):
        lo_t_ref[c] = jnp.where(hi_t_ref[c] == hi_sel16, lo_t_ref[c], jnp.full((TC, TQ), I16_MIN, I16))
        return carry

    lax.fori_loop(0, n_ch, lo_body, 0)
    lo_sel = search16(lo_t_ref, kf - above)
    lo_sel16 = lo_sel.astype(I16)
    thr_t = (hi_sel << 16) | (lo_sel - I16_MIN)
    thr_t = jnp.maximum(thr_t, INT_MIN + 1)
    need_t = kf - above - count16(lo_t_ref, lambda a: a > lo_sel16)
    ties_t = count16(lo_t_ref, lambda a: a == lo_sel16)
    all_ties_fit = jnp.min(jnp.where(need_t >= ties_t, 1, 0)) > 0
    thr = jnp.broadcast_to(thr_t, (LANES, TQ)).T
    thr_w = _rep(thr, n_lt)

    def bias_plain():
        def body(c, carry):
            bias_ref[c] = jnp.where(keys_ref[c] >= thr_w, 0.0, NEG_INF)
            return carry
        lax.fori_loop(0, n_ch, body, 0)

    def bias_with_ties():
        need_w = _rep(jnp.broadcast_to(need_t, (LANES, TQ)).T, n_lt)

        def body(c, run):
            key = keys_ref[c]
            eq = key == thr_w
            eqf = jnp.where(eq, 1.0, 0.0).astype(MXU_DTYPE)
            prefix = _dot(eqf, tri_ref[...])
            take = (key > thr_w) | (eq & (_rep(run, n_lt) + prefix <= need_w))
            bias_ref[c] = jnp.where(take, 0.0, NEG_INF)
            return run + _dot(eqf, ones_ref[...])
        lax.fori_loop(0, n_ch, body, jnp.zeros((TQ, LANES), F32))

    lax.cond(all_ties_fit, bias_plain, bias_with_ties)

    def att_body(c, carry):
        start = pl.multiple_of(c * TC, TC)
        kc = kcat_ref[0, pl.ds(start, TC), :]
        _attn_chunk(lambda h0, h1: qcat_ref[h0 * TQ:h1 * TQ, :], kc, kc, bias_ref[c], m_ref, acc_ref)
        return carry

    _attn_init(m_ref, acc_ref)
    lax.fori_loop(0, n_ch, att_body, 0)

    for pair in range(DSA_HEADS // 2):
        outs = []
        for hd in (2 * pair, 2 * pair + 1):
            a = acc_ref[hd]
            outs.append(a[:, 0:LANES] / jnp.maximum(a[:, ONES_LANE:ONES_LANE + 1], 1e-30))
        both = jnp.concatenate(outs, axis=1).astype(MXU_DTYPE)
        o_ref[0, :, pair * LANES:(pair + 1) * LANES] = _dot(both, wuv_ref[pair]).astype(o_ref.dtype)


def _dsa_call(dq, iq, misc, kidx, kcat, wq, wuv, tri, ones, wexp):
    bsz, s, _ = dq.shape
    n_ch = s // TC
    k_sel = min(IDX_TOPK_MAX, s // 4)
    TQ = TQ_DSA
    rowspec = lambda width: pl.BlockSpec((1, TQ, width), lambda b, i: (b, i, 0))
    batch = lambda a: pl.BlockSpec((1,) + a.shape[1:], lambda b, i: (b, 0, 0))
    const = lambda a: pl.BlockSpec(a.shape, lambda b, i: (0,) * a.ndim)
    stat = pltpu.VMEM((DSA_HEADS, TQ, LANES), F32)
    return pl.pallas_call(
        functools.partial(_dsa_kernel, k_sel=k_sel),
        grid=(bsz, s // TQ),
        in_specs=[rowspec(512), rowspec(512), rowspec(LANES), batch(kidx), batch(kcat),
                  const(wq), const(wuv), const(tri), const(ones), const(wexp)],
        out_specs=rowspec(512),
        out_shape=jax.ShapeDtypeStruct((bsz, s, 512), MXU_DTYPE),
        scratch_shapes=[pltpu.VMEM((n_ch, TQ, TC), I32),
                        pltpu.VMEM((n_ch, TC, TQ), I16),
                        pltpu.VMEM((n_ch, TC, TQ), I16),
                        pltpu.VMEM((n_ch, TQ, TC), F32),
                        pltpu.VMEM((DSA_HEADS * TQ, 2 * LANES), MXU_DTYPE),
                        pltpu.VMEM((IDX_HEADS // 2 * TQ, LANES), MXU_DTYPE),
                        stat,
                        stat,
                        pltpu.VMEM((DSA_HEADS, TQ, 2 * LANES), F32)],
        compiler_params=pltpu.CompilerParams(vmem_limit_bytes=VMEM_LIMIT),
        name="dsa_attention",
    )(dq, iq, misc, kidx, kcat, wq, wuv, tri, ones, wexp)


def _rms(v, g):
    return v * lax.rsqrt(jnp.mean(v * v, axis=-1, keepdims=True) + EPS) * g


def _ffn_kernel(x_ref, oa_ref, ob_ref, mod_ref, gpm_ref, gpf_ref, gqf_ref, wout_ref, wgu_ref, wd_ref,
                o_ref, *, d_ff):
    x = x_ref[0]
    half = oa_ref.shape[2]
    y = _dot(oa_ref[0], wout_ref[0:half, :]) + _dot(ob_ref[0], wout_ref[half:2 * half, :])
    x1 = x + mod_ref[0, 2:3, :] * _rms(y, gpm_ref[...])
    h = (_rms(x1, gpf_ref[...]) * (1.0 + mod_ref[0, 4:5, :]) + mod_ref[0, 3:4, :]).astype(MXU_DTYPE)
    acc = jnp.zeros(x.shape, F32)
    for c in range(d_ff // FF_CHUNK):
        lo = c * FF_CHUNK
        gate = _dot(h, wgu_ref[:, lo:lo + FF_CHUNK])
        up = _dot(h, wgu_ref[:, d_ff + lo:d_ff + lo + FF_CHUNK])
        act = (gate * _sigmoid(gate) * up).astype(MXU_DTYPE)
        acc = acc + _dot(act, wd_ref[lo:lo + FF_CHUNK, :])
    o_ref[0] = x1 + mod_ref[0, 5:6, :] * _rms(acc, gqf_ref[...])


def _ffn_call(x, o_a, o_b, mod3, g_post_mix, g_pre_ffn, g_post_ffn, w_out, w_gu, w_down):
    bsz, s, d = x.shape
    d_ff = w_down.shape[0]
    tm = min(TM_FFN, s)
    row = lambda width: pl.BlockSpec((1, tm, width), lambda b, i: (b, i, 0))
    const = lambda a: pl.BlockSpec(a.shape, lambda b, i: (0, 0), pipeline_mode=pl.Buffered(1))
    return pl.pallas_call(
        functools.partial(_ffn_kernel, d_ff=d_ff),
        grid=(bsz, s // tm),
        in_specs=[row(d), row(o_a.shape[2]), row(o_b.shape[2]),
                  pl.BlockSpec((1, 6, d), lambda b, i: (b, 0, 0)),
                  const(g_post_mix), const(g_pre_ffn), const(g_post_ffn),
                  const(w_out), const(w_gu), const(w_down)],
        out_specs=row(d),
        out_shape=jax.ShapeDtypeStruct((bsz, s, d), x.dtype),
        compiler_params=pltpu.CompilerParams(vmem_limit_bytes=VMEM_LIMIT),
        name="outproj_ffn",
    )(x, o_a, o_b, mod3, g_post_mix, g_pre_ffn, g_post_ffn, w_out, w_gu, w_down)


def _in_perm_indices():
    splits = (NSA_HEADS * HEAD_DIM,) + (NSA_KV_HEADS * HEAD_DIM,) * 6 + (
        3 * NSA_HEADS, DSA_HEADS * HEAD_DIM, KV_LORA, ROPE_DIM, IDX_HEADS * IDX_DIM, IDX_DIM, IDX_HEADS)
    off = np.concatenate([[0], np.cumsum(splits)])
    seg = lambda i: np.arange(off[i], off[i + 1])
    (nq, nkc, nvc, nks, nvs, nkw, nvw, ngate, dq, dckv, dkr, iq, ik, iw) = [seg(i) for i in range(14)]
    nq = nq[_nsa_out_rows()]
    return np.concatenate([nq, nkc, nvc, nks, nvs, nkw, nvw, dq, iq, dckv, ik, dkr, ngate, iw])


def _rope_tables(s):
    inv_freq = 1.0 / (ROPE_THETA ** (jnp.arange(0, ROPE_DIM, 2, dtype=F32) / ROPE_DIM))
    ang = jnp.arange(s, dtype=F32)[:, None] * inv_freq[None, :]
    cos, sin = jnp.cos(ang), jnp.sin(ang)
    ones = jnp.ones((s, HEAD_DIM - ROPE_DIM), F32)
    zeros8 = jnp.zeros((s, ROPE_HALF), F32)
    zeros = jnp.zeros((s, HEAD_DIM - ROPE_DIM), F32)
    cos_h = jnp.concatenate([cos, cos, ones], axis=1)
    sa_h = jnp.concatenate([-sin, zeros8, zeros], axis=1)
    sb_h = jnp.concatenate([zeros8, sin, zeros], axis=1)
    two = lambda t: jnp.concatenate([t, t], axis=1)
    return two(cos_h), two(sa_h), two(sb_h)


def _blockdiag2(a, b):
    za = jnp.zeros((a.shape[0], b.shape[1]), a.dtype)
    zb = jnp.zeros((b.shape[0], a.shape[1]), a.dtype)
    return jnp.concatenate([jnp.concatenate([a, za], axis=1), jnp.concatenate([zb, b], axis=1)], axis=0)


def _layer_weights(w_in, cmp_pe, cmp_w1, cmp_b1, cmp_w2, cmp_b2, w_uk, w_uv, s):
    d = w_in.shape[0]
    perm = _in_perm_indices()
    w_perm = jnp.concatenate([w_in[:, perm], jnp.zeros((d, IN_PERM_WIDTH - perm.size), w_in.dtype)], axis=1)
    w1 = cmp_w1.reshape(2, CMP_LEN, HEAD_DIM, CMP_HIDDEN)
    bd = jax.vmap(jax.vmap(lambda m: _blockdiag2(m, m)))
    w1a, w1b = bd(w1[:, :CMP_STRIDE]), bd(w1[:, CMP_STRIDE:])
    pe2 = jnp.concatenate([cmp_pe, cmp_pe], axis=-1)[:, :, None, :]
    pea, peb = pe2[:, :CMP_STRIDE], pe2[:, CMP_STRIDE:]
    b1 = jnp.concatenate([cmp_b1, cmp_b1], axis=-1)[:, None, :]
    w2 = jax.vmap(lambda m: _blockdiag2(m, m))(cmp_w2)
    b2 = jnp.concatenate([cmp_b2, cmp_b2], axis=-1)[:, None, :]
    def head_q(uk):
        top = jnp.concatenate([jnp.zeros((ROPE_DIM, KV_LORA), F32), jnp.eye(ROPE_DIM, dtype=F32),
                               jnp.zeros((ROPE_DIM, LANES - ROPE_DIM), F32)], axis=1)
        bot = jnp.concatenate([uk.T, jnp.zeros((NOPE_DIM, LANES), F32)], axis=1)
        return jnp.concatenate([top, bot], axis=0)
    hq = jax.vmap(head_q)(w_uk)
    wq = jax.vmap(_blockdiag2)(hq[0::2], hq[1::2])
    wuv = jax.vmap(_blockdiag2)(w_uv[0::2], w_uv[1::2])
    c = lambda a: a.astype(MXU_DTYPE)
    return dict(w_perm=c(w_perm), w1a=c(w1a), w1b=c(w1b), pea=pea, peb=peb, b1=b1, w2=c(w2), b2=b2,
                wq=c(wq), wuv=c(wuv))


def _selection_constants(s):
    n_cmp = (s - CMP_LEN) // CMP_STRIDE + 1
    n_slc = s // SLC_BLOCK
    ci = np.arange(n_cmp)[:, None] * CMP_STRIDE
    sj = np.arange(n_slc)[None, :] * SLC_BLOCK
    ovt = np.zeros((LANES, s // CMP_STRIDE), np.float32)
    ovt[:n_slc, :n_cmp] = ((ci < sj + SLC_BLOCK) & (ci + CMP_LEN > sj)).astype(np.float32).T
    tri = np.triu(np.ones((TC, TC), np.float32))
    ones = np.ones((TC, LANES), np.float32)
    gexp = np.zeros((LANES, 3 * NSA_GROUP * LANES), np.float32)
    for br in range(3):
        for hd in range(NSA_HEADS):
            lo = (br * NSA_GROUP + hd % NSA_GROUP) * LANES + (hd // NSA_GROUP) * (LANES // 2)
            gexp[MISC_GATE + br * NSA_HEADS + hd, lo:lo + LANES // 2] = 1.0
    gexp = np.concatenate([gexp, gexp], axis=0)
    wexp = np.zeros((LANES, IDX_HEADS * LANES), np.float32)
    for hd in range(IDX_HEADS):
        wexp[MISC_IW + hd, hd * LANES:(hd + 1) * LANES] = 1.0
    wexp = np.concatenate([wexp, wexp], axis=0)
    c = lambda a: jnp.asarray(a).astype(MXU_DTYPE)
    return c(ovt), c(tri), c(ones), c(gexp), c(wexp)


def _nsa_out_rows():
    order = [hd for p in range(NSA_GROUP) for hd in (p, p + NSA_GROUP)]
    return np.concatenate([np.arange(hd * HEAD_DIM, (hd + 1) * HEAD_DIM) for hd in order])


def kernel(x, c, w_ada, b_ada, g_pre_mix, g_post_mix, g_pre_ffn, g_post_ffn, w_in, cmp_pe, cmp_w1, cmp_b1,
           cmp_w2, cmp_b2, g_kv_norm, w_uk, w_uv, w_out, w_gate_up, w_down):
    bsz, s, d = x.shape
    depth = w_ada.shape[0]
    assert s % TM_PROJ == 0 and s % TC == 0 and s // CMP_STRIDE == LANES and s // SLC_BLOCK <= LANES
    assert WINDOW % TQ == 0 and s >= WINDOW + TQ
    cos_t, sa_t, sb_t = _rope_tables(s)
    ovt, tri, ones, gexp, wexp = _selection_constants(s)
    out_rows = np.concatenate([_nsa_out_rows(), np.arange(NSA_HEADS * HEAD_DIM, d)])
    for l in range(depth):
        lw = _layer_weights(w_in[l], cmp_pe[l], cmp_w1[l], cmp_b1[l], cmp_w2[l], cmp_b2[l], w_uk[l], w_uv[l], s)
        mod3 = _ada_call(c, w_ada[l], b_ada[l]).reshape(bsz, 6, d)
        nq, ncv, nkv, dq, iq, kcat, kidx, misc = _inproj_call(
            x, mod3, g_pre_mix[l][None], g_kv_norm[l][None], lw["w_perm"], cos_t, sa_t, sb_t)
        kcv = _compress_call(ncv, lw["w1a"], lw["w1b"], lw["pea"], lw["peb"], lw["b1"], lw["w2"], lw["b2"])
        o_a = _nsa_call(nq, nkv, kcv, misc, ovt, gexp)
        o_b = _dsa_call(dq, iq, misc, kidx, kcat, lw["wq"], lw["wuv"], tri, ones, wexp)
        x = _ffn_call(x, o_a, o_b, mod3, g_post_mix[l][None], g_pre_ffn[l][None], g_post_ffn[l][None],
                      w_out[l][out_rows].astype(MXU_DTYPE), w_gate_up[l].astype(MXU_DTYPE),
                      w_down[l].astype(MXU_DTYPE))
    return x
```
